```python
import jax, jax.numpy as jnp
from jax import lax
import numpy as np

D_MODEL = 1024
BATCH = 16
SEQ = 4096
DEPTH = 4

GLA_HEADS = 4
GLA_DK = 64
GLA_DV = 128
GLA_RANK = 16
GLA_TAU = 16.0
GLA_CHUNK = 64
SGU_GROUPS = 4
SGU_DC = 128
SGU_CHUNK = 128
D_FF = 4 * D_MODEL
EPS = 1e-6

GLA_QK = GLA_HEADS * GLA_DK
GLA_V = GLA_HEADS * GLA_DV
SGU_W = SGU_GROUPS * SGU_DC
D_MIX = GLA_V + SGU_W
IN_SIZES = (GLA_QK, GLA_QK, GLA_V, GLA_V, GLA_RANK, GLA_RANK, SGU_W, SGU_W)
D_IN = GLA_QK * 2 + GLA_V * 2 + GLA_RANK * 2 + SGU_W * 2
SPLIT_POINTS = (GLA_QK, 2 * GLA_QK, 2 * GLA_QK + GLA_V, 2 * GLA_QK + 2 * GLA_V,
                2 * GLA_QK + 2 * GLA_V + GLA_RANK, 2 * GLA_QK + 2 * GLA_V + 2 * GLA_RANK,
                2 * GLA_QK + 2 * GLA_V + 2 * GLA_RANK + SGU_W)

kernel_name = "hybrid_gla_sgu_encoder"


def rmsnorm(x, g):
    xf = x.astype(jnp.float32)
    y = xf * lax.rsqrt(jnp.mean(xf * xf, axis=-1, keepdims=True) + EPS)
    return (y * g.astype(jnp.float32)).astype(x.dtype)


def gla_one_direction(q, k, v, log_a, strict):
    B, S, H, K = q.shape
    V = v.shape[-1]
    C = GLA_CHUNK
    N = S // C
    q = q.reshape(B, N, C, H, K)
    k = k.reshape(B, N, C, H, K)
    v = v.reshape(B, N, C, H, V)
    b = jnp.cumsum(log_a.reshape(B, N, C, H, K), axis=2)
    b_mid = b[:, :, C // 2:C // 2 + 1]
    q_in = q * jnp.exp(b - b_mid)
    k_in = k * jnp.exp(b_mid - b)
    scores = jnp.einsum('bnchk,bnjhk->bnhcj', q_in, k_in)
    idx = jnp.arange(C)
    mask = (idx[:, None] > idx[None, :]) if strict else (idx[:, None] >= idx[None, :])
    scores = jnp.where(mask, scores, 0.0)
    o_intra = jnp.einsum('bnhcj,bnjhv->bnchv', scores, v)
    b_last = b[:, :, -1]
    kv = jnp.einsum('bnchk,bnchv->bnhkv', k * jnp.exp(b_last[:, :, None] - b), v)

    def step(state, inp):
        dec, kv_n = inp
        return dec[..., None] * state + kv_n, state

    init = jnp.zeros((B, H, K, V), q.dtype)
    _, s_in = lax.scan(step, init, (jnp.moveaxis(jnp.exp(b_last), 1, 0), jnp.moveaxis(kv, 1, 0)))
    s_in = jnp.moveaxis(s_in, 0, 1)
    o_inter = jnp.einsum('bnchk,bnhkv->bnchv', q * jnp.exp(b), s_in)
    return (o_intra + o_inter).reshape(B, S, H, V)


def gla_mixer(q, k, v, g, a_f, a_b, w_a2_f, b_a_f, w_a2_b, b_a_b, norm_g):
    B, S, _ = q.shape
    f32 = jnp.float32
    q = q.astype(f32).reshape(B, S, GLA_HEADS, GLA_DK) * (GLA_DK ** -0.5)
    k = k.astype(f32).reshape(B, S, GLA_HEADS, GLA_DK)
    v = v.astype(f32).reshape(B, S, GLA_HEADS, GLA_DV)
    log_a_f = (jax.nn.log_sigmoid(a_f.astype(f32) @ w_a2_f.astype(f32) + b_a_f.astype(f32)) / GLA_TAU
               ).reshape(B, S, GLA_HEADS, GLA_DK)
    log_a_b = (jax.nn.log_sigmoid(a_b.astype(f32) @ w_a2_b.astype(f32) + b_a_b.astype(f32)) / GLA_TAU
               ).reshape(B, S, GLA_HEADS, GLA_DK)
    o_f = gla_one_direction(q, k, v, log_a_f, False)
    flip = lambda t: jnp.flip(t, axis=1)
    o_b = flip(gla_one_direction(flip(q), flip(k), flip(v), flip(log_a_b), True))
    o = o_f + o_b
    o = o * lax.rsqrt(jnp.mean(o * o, axis=-1, keepdims=True) + EPS) * norm_g.astype(f32)
    return o.reshape(B, S, GLA_V) * jax.nn.silu(g.astype(f32))


def sgu_mixer(u, v, norm_g, w_s, b_s):
    B, S, _ = u.shape
    f32 = jnp.float32
    N = S // SGU_CHUNK
    v = v.astype(f32).reshape(B, N, SGU_CHUNK, SGU_GROUPS, SGU_DC)
    v = v * lax.rsqrt(jnp.mean(v * v, axis=-1, keepdims=True) + EPS) * norm_g.astype(f32)
    mixed = jnp.einsum('gpq,bnqgc->bnpgc', w_s.astype(f32), v) + b_s.astype(f32).T[:, :, None]
    return u.astype(f32) * mixed.reshape(B, S, SGU_W)


def setup_inputs(seed: int = 0) -> dict:
    key = jax.random.key(seed)
    ks = jax.random.split(key, 20)
    nrm = lambda k, shape, s: jax.random.normal(k, shape, jnp.float32) * s
    L = DEPTH
    return {
        "x": nrm(ks[0], (BATCH, SEQ, D_MODEL), 1.0),
        "norm_mix_g": 1.0 + nrm(ks[1], (L, D_MODEL), 0.02),
        "w_in": nrm(ks[2], (L, D_MODEL, D_IN), D_MODEL ** -0.5),
        "w_a2_fwd": nrm(ks[3], (L, GLA_RANK, GLA_QK), GLA_RANK ** -0.5),
        "b_a_fwd": 1.5 + nrm(ks[4], (L, GLA_QK), 1.0),
        "w_a2_bwd": nrm(ks[5], (L, GLA_RANK, GLA_QK), GLA_RANK ** -0.5),
        "b_a_bwd": 1.5 + nrm(ks[6], (L, GLA_QK), 1.0),
        "gla_norm_g": 1.0 + nrm(ks[7], (L, GLA_HEADS, GLA_DV), 0.02),
        "sgu_norm_g": 1.0 + nrm(ks[8], (L, SGU_GROUPS, SGU_DC), 0.02),
        "w_s": nrm(ks[9], (L, SGU_GROUPS, SGU_CHUNK, SGU_CHUNK), SGU_CHUNK ** -0.5),
        "b_s": 1.0 + nrm(ks[10], (L, SGU_GROUPS, SGU_CHUNK), 0.1),
        "w_out": nrm(ks[11], (L, D_MIX, D_MODEL), D_MIX ** -0.5),
        "norm_mlp_g": 1.0 + nrm(ks[12], (L, D_MODEL), 0.02),
        "w_mlp1": nrm(ks[13], (L, D_MODEL, D_FF), D_MODEL ** -0.5),
        "w_mlp2": nrm(ks[14], (L, D_FF, D_MODEL), D_FF ** -0.5),
        "final_norm_g": 1.0 + nrm(ks[15], (D_MODEL,), 0.02),
    }


def reference(x, norm_mix_g, w_in, w_a2_fwd, b_a_fwd, w_a2_bwd, b_a_bwd, gla_norm_g,
              sgu_norm_g, w_s, b_s, w_out, norm_mlp_g, w_mlp1, w_mlp2, final_norm_g):
    for l in range(DEPTH):
        h = rmsnorm(x, norm_mix_g[l])
        z = h @ w_in[l]
        q, k, v, g, a_f, a_b, su, sv = jnp.split(z, SPLIT_POINTS, axis=-1)
        o_gla = gla_mixer(q, k, v, g, a_f, a_b, w_a2_fwd[l], b_a_fwd[l],
                          w_a2_bwd[l], b_a_bwd[l], gla_norm_g[l]).astype(x.dtype)
        o_sgu = sgu_mixer(jax.nn.gelu(su, approximate=False), jax.nn.gelu(sv, approximate=False),
                          sgu_norm_g[l], w_s[l], b_s[l]).astype(x.dtype)
        x = x + jnp.concatenate([o_gla, o_sgu], axis=-1) @ w_out[l]
        h = rmsnorm(x, norm_mlp_g[l])
        x = x + jnp.square(jax.nn.relu(h @ w_mlp1[l])) @ w_mlp2[l]
    return rmsnorm(x, final_norm_g)
```

```python
import functools

import jax
import jax.numpy as jnp
from jax import lax
from jax.experimental import pallas as pl
from jax.experimental.pallas import tpu as pltpu

f32 = jnp.float32
bf16 = jnp.bfloat16

D_MODEL = 1024
GLA_HEADS = 4
GLA_DK = 64
GLA_DV = 128
GLA_RANK = 16
GLA_TAU = 16.0
GLA_CHUNK = 64
SGU_GROUPS = 4
SGU_DC = 128
SGU_CHUNK = 128
D_FF = 4 * D_MODEL
EPS = 1e-6

GLA_QK = GLA_HEADS * GLA_DK
GLA_V = GLA_HEADS * GLA_DV
SGU_W = SGU_GROUPS * SGU_DC
LANES = 128
A_PAD = LANES
HEAD_PAIRS = GLA_HEADS // 2
PAIR_K = 2 * GLA_DK
PAIR_V = 2 * GLA_DV

_PROJ_COLS = (("q", GLA_QK), ("k", GLA_QK), ("v", GLA_V), ("g", GLA_V),
              ("su", SGU_W), ("sv", SGU_W), ("a", A_PAD))
D_PROJ = sum(w for _, w in _PROJ_COLS)

VMEM_LIMIT_BYTES = 56 * 1024 * 1024

PROJ_TM = 1024
GLA_TS = 256
CUMSUM_TILE = 256
MLP_TM = 512
FF_CHUNK = 1024


def _rms(x, g):
    return x * lax.rsqrt(jnp.mean(x * x, axis=-1, keepdims=True) + EPS) * g


def _gelu(x):
    return 0.5 * x * (1.0 + lax.erf(x * (2.0 ** -0.5)))


def _dot(a, b):
    return jnp.dot(a, b, preferred_element_type=f32)


def _proj_kernel(x_ref, g_ref, w_ref, *out_refs):
    h = _rms(x_ref[...], g_ref[...]).astype(bf16)
    lo = 0
    for ref, (_, width) in zip(out_refs, _PROJ_COLS):
        ref[...] = _dot(h, w_ref[:, lo:lo + width]).astype(ref.dtype)
        lo += width


def _proj(x, g, w):
    T = x.shape[0]
    tm = min(PROJ_TM, T)
    row = lambda width: pl.BlockSpec((tm, width), lambda i: (i, 0))
    const = lambda shape: pl.BlockSpec(shape, lambda i: (0, 0))
    return pl.pallas_call(
        _proj_kernel,
        grid=(T // tm,),
        in_specs=[row(D_MODEL), const((1, D_MODEL)), const((D_MODEL, D_PROJ))],
        out_specs=[row(width) for _, width in _PROJ_COLS],
        out_shape=[jax.ShapeDtypeStruct((T, width), bf16) for _, width in _PROJ_COLS],
        compiler_params=pltpu.CompilerParams(
            dimension_semantics=("arbitrary",), vmem_limit_bytes=VMEM_LIMIT_BYTES),
        name="proj",
    )(x, g, w)


def _gla_kernel(reverse, q_ref, k_ref, v_ref, a_ref, wa_ref, ba_ref, *rest):
    if reverse:
        out_ref, s_ref, o_scr = rest
    else:
        ob_ref, gate_ref, ng_ref, out_ref, s_ref, o_scr = rest
    C = GLA_CHUNK
    TS = q_ref.shape[0]
    NC = TS // C

    @pl.when(pl.program_id(1) == 0)
    def _():
        s_ref[...] = jnp.zeros_like(s_ref)

    pre = _dot(a_ref[...], wa_ref[...]) + ba_ref[...]
    log_a = jax.nn.log_sigmoid(pre) * (1.0 / GLA_TAU)
    ct = min(CUMSUM_TILE, TS)
    r = lax.broadcasted_iota(jnp.int32, (ct, ct), 0)
    c = lax.broadcasted_iota(jnp.int32, (ct, ct), 1)
    tri = (c >= r) if reverse else (c <= r)
    l_bd = jnp.where(((r // C) == (c // C)) & tri, 1.0, 0.0).astype(bf16)
    hi = log_a.astype(bf16)
    lo = (log_a - hi.astype(f32)).astype(bf16)
    b = jnp.concatenate(
        [_dot(l_bd, hi[i:i + ct]) + _dot(l_bd, lo[i:i + ct]) for i in range(0, TS, ct)], axis=0)
    b = b.reshape(NC, C, GLA_QK)
    mid, last = (C - 1 - C // 2, 0) if reverse else (C // 2, C - 1)
    b_mid = b[:, mid:mid + 1, :]
    b_last = b[:, last:last + 1, :]
    q = q_ref[...].astype(f32).reshape(NC, C, GLA_QK) * (GLA_DK ** -0.5)
    k = k_ref[...].astype(f32).reshape(NC, C, GLA_QK)
    q_in = (q * jnp.exp(b - b_mid)).astype(bf16)
    k_in = (k * jnp.exp(b_mid - b)).astype(bf16)
    k_dec = (k * jnp.exp(b_last - b)).astype(bf16)
    q_st = (q * jnp.exp(b)).astype(bf16)
    dec = jnp.exp(b_last)

    lane = lax.broadcasted_iota(jnp.int32, (C, PAIR_K), 1)
    row = lax.broadcasted_iota(jnp.int32, (C, PAIR_K), 0)
    even_head = lane < GLA_DK
    j = lane % GLA_DK
    causal = (j > row) if reverse else (j <= row)
    zero_v = jnp.zeros((C, GLA_DV), bf16)

    for n in (range(NC - 1, -1, -1) if reverse else range(NC)):
        rows = slice(n * C, (n + 1) * C)
        for p in range(HEAD_PAIRS):
            ks = slice(p * PAIR_K, (p + 1) * PAIR_K)
            kin = k_in[n][:, ks]
            k_sep = jnp.concatenate(
                [jnp.where(even_head, kin, 0), jnp.where(even_head, 0, kin)], axis=0)
            sc = lax.dot_general(q_in[n][:, ks], k_sep, (((1,), (1,)), ((), ())),
                                 preferred_element_type=f32)
            sc = jnp.where(causal, sc, 0.0).astype(bf16)
            vp = v_ref[rows, p * PAIR_V:(p + 1) * PAIR_V]
            v_bd = jnp.concatenate(
                [jnp.concatenate([vp[:, :GLA_DV], zero_v], axis=1),
                 jnp.concatenate([zero_v, vp[:, GLA_DV:]], axis=1)], axis=0)
            s = s_ref[p]
            lhs = jnp.concatenate([sc, q_st[n][:, ks]], axis=1)
            rhs = jnp.concatenate([v_bd, s.astype(bf16)], axis=0)
            o_scr[rows, p * PAIR_V:(p + 1) * PAIR_V] = _dot(lhs, rhs)
            kv = lax.dot_general(k_dec[n][:, ks], vp, (((0,), (0,)), ((), ())),
                                 preferred_element_type=f32)
            dcol = jnp.broadcast_to(dec[n][:, ks], (PAIR_K, PAIR_K)).T
            s_ref[p, 0:GLA_DK, 0:GLA_DV] = (
                s[0:GLA_DK, 0:GLA_DV] * dcol[0:GLA_DK] + kv[0:GLA_DK, 0:GLA_DV])
            s_ref[p, GLA_DK:, GLA_DV:] = (
                s[GLA_DK:, GLA_DV:] * dcol[GLA_DK:] + kv[GLA_DK:, GLA_DV:])

    if reverse:
        out_ref[...] = o_scr[...]
    else:
        o = o_scr[...] + ob_ref[...]
        gate = gate_ref[...].astype(f32)
        for h in range(GLA_HEADS):
            hs = slice(h * GLA_DV, (h + 1) * GLA_DV)
            gh = gate[:, hs]
            out_ref[:, hs] = (_rms(o[:, hs], ng_ref[:, hs]) * (gh * jax.nn.sigmoid(gh))).astype(bf16)


def _gla(reverse, B, S, q, k, v, a, wa, ba, extra=()):
    T = q.shape[0]
    ts = min(GLA_TS, S)
    nt = S // ts
    if reverse:
        tile = lambda b, t: (b * nt + (nt - 1 - t), 0)
    else:
        tile = lambda b, t: (b * nt + t, 0)
    row = lambda width: pl.BlockSpec((ts, width), tile)
    const = lambda shape: pl.BlockSpec(shape, lambda b, t: (0, 0))
    in_specs = [row(GLA_QK), row(GLA_QK), row(GLA_V), row(A_PAD),
                const((A_PAD, GLA_QK)), const((1, GLA_QK))]
    if not reverse:
        in_specs += [row(GLA_V), row(GLA_V), const((1, GLA_V))]
    return pl.pallas_call(
        functools.partial(_gla_kernel, reverse),
        grid=(B, nt),
        in_specs=in_specs,
        out_specs=row(GLA_V),
        out_shape=jax.ShapeDtypeStruct((T, GLA_V), f32 if reverse else bf16),
        scratch_shapes=[pltpu.VMEM((HEAD_PAIRS, PAIR_K, PAIR_V), f32),
                        pltpu.VMEM((ts, GLA_V), f32)],
        compiler_params=pltpu.CompilerParams(
            dimension_semantics=("arbitrary", "arbitrary"), vmem_limit_bytes=VMEM_LIMIT_BYTES),
        name="gla_bwd" if reverse else "gla_fwd",
    )(q, k, v, a, wa, ba, *extra)


def _mix_mlp_kernel(final, x_ref, og_ref, su_ref, sv_ref, sg_ref, ws_ref, bs_ref, wo_ref,
                    nm_ref, w1_ref, w2_ref, *rest):
    if final:
        fg_ref, out_ref = rest
    else:
        (out_ref,) = rest
    TM = x_ref.shape[0]
    P = SGU_CHUNK
    su = _gelu(su_ref[...].astype(f32))
    sv = _gelu(sv_ref[...].astype(f32))
    mixed = []
    for g in range(SGU_GROUPS):
        gs = slice(g * SGU_DC, (g + 1) * SGU_DC)
        vn = _rms(sv[:, gs], sg_ref[:, gs]).astype(bf16)
        mixed.append(jnp.concatenate(
            [_dot(ws_ref[g], vn[m:m + P]) for m in range(0, TM, P)], axis=0))
    mixed = jnp.concatenate(mixed, axis=1).reshape(TM // P, P, SGU_W) + bs_ref[...]
    o_sgu = (su * mixed.reshape(TM, SGU_W)).astype(bf16)
    out_ref[...] = x_ref[...] + _dot(og_ref[...], wo_ref[:GLA_V, :]) + _dot(o_sgu, wo_ref[GLA_V:, :])
    x1 = out_ref[...]
    h = _rms(x1, nm_ref[...]).astype(bf16)
    mlp = None
    for f in range(0, D_FF, FF_CHUNK):
        u = jnp.maximum(_dot(h, w1_ref[:, f:f + FF_CHUNK]), 0.0)
        t = _dot((u * u).astype(bf16), w2_ref[f:f + FF_CHUNK, :])
        mlp = t if mlp is None else mlp + t
    x2 = x1 + mlp
    out_ref[...] = _rms(x2, fg_ref[...]) if final else x2


def _mix_mlp(x, og, su, sv, sg, ws, bs, wo, nm, w1, w2, fg=None):
    T = x.shape[0]
    tm = min(MLP_TM, T)
    final = fg is not None
    row = lambda width: pl.BlockSpec((tm, width), lambda i: (i, 0))

    def const(shape):
        return pl.BlockSpec(shape, lambda i: (0,) * len(shape), pipeline_mode=pl.Buffered(1))

    in_specs = [row(D_MODEL), row(GLA_V), row(SGU_W), row(SGU_W),
                const((1, SGU_W)), const((SGU_GROUPS, SGU_CHUNK, SGU_CHUNK)),
                const((SGU_CHUNK, SGU_W)), const((D_MODEL, D_MODEL)), const((1, D_MODEL)),
                const((D_MODEL, D_FF)), const((D_FF, D_MODEL))]
    args = [x, og, su, sv, sg, ws, bs, wo, nm, w1, w2]
    if final:
        in_specs.append(const((1, D_MODEL)))
        args.append(fg)
    return pl.pallas_call(
        functools.partial(_mix_mlp_kernel, final),
        grid=(T // tm,),
        in_specs=in_specs,
        out_specs=row(D_MODEL),
        out_shape=jax.ShapeDtypeStruct((T, D_MODEL), f32),
        compiler_params=pltpu.CompilerParams(
            dimension_semantics=("arbitrary",), vmem_limit_bytes=VMEM_LIMIT_BYTES),
        name="mix_mlp_final" if final else "mix_mlp",
    )(*args)


def _permute_w_in(w_in):
    L = w_in.shape[0]
    q, k, v, g, a_f, a_b, su, sv = jnp.split(
        w_in, (GLA_QK, 2 * GLA_QK, 2 * GLA_QK + GLA_V, 2 * GLA_QK + 2 * GLA_V,
               2 * GLA_QK + 2 * GLA_V + GLA_RANK, 2 * GLA_QK + 2 * GLA_V + 2 * GLA_RANK,
               2 * GLA_QK + 2 * GLA_V + 2 * GLA_RANK + SGU_W), axis=-1)
    pad = jnp.zeros((L, D_MODEL, A_PAD - 2 * GLA_RANK), w_in.dtype)
    return jnp.concatenate([q, k, v, g, su, sv, a_f, a_b, pad], axis=-1).astype(bf16)


def _pad_decay_w(w_a2, offset):
    L = w_a2.shape[0]
    out = jnp.zeros((L, A_PAD, GLA_QK), w_a2.dtype)
    return out.at[:, offset:offset + GLA_RANK, :].set(w_a2).astype(bf16)


def kernel(x, norm_mix_g, w_in, w_a2_fwd, b_a_fwd, w_a2_bwd, b_a_bwd, gla_norm_g, sgu_norm_g,
           w_s, b_s, w_out, norm_mlp_g, w_mlp1, w_mlp2, final_norm_g):
    B, S, D = x.shape
    L = w_in.shape[0]
    assert D == D_MODEL and S % max(GLA_CHUNK, SGU_CHUNK) == 0
    T = B * S

    w_in_p = _permute_w_in(w_in)
    wa_f = _pad_decay_w(w_a2_fwd, 0)
    wa_b = _pad_decay_w(w_a2_bwd, GLA_RANK)
    ws_b = w_s.astype(bf16)
    bs_full = jnp.repeat(jnp.swapaxes(b_s, 1, 2), SGU_DC, axis=2)
    wo_b = w_out.astype(bf16)
    w1_b = w_mlp1.astype(bf16)
    w2_b = w_mlp2.astype(bf16)

    xf = x.reshape(T, D)
    for l in range(L):
        q, k, v, g, su, sv, a = _proj(xf, norm_mix_g[l].reshape(1, D), w_in_p[l])
        o_b = _gla(True, B, S, q, k, v, a, wa_b[l], b_a_bwd[l].reshape(1, GLA_QK))
        og = _gla(False, B, S, q, k, v, a, wa_f[l], b_a_fwd[l].reshape(1, GLA_QK),
                  extra=(o_b, g, gla_norm_g[l].reshape(1, GLA_V)))
        xf = _mix_mlp(xf, og, su, sv, sgu_norm_g[l].reshape(1, SGU_W), ws_b[l], bs_full[l],
                      wo_b[l], norm_mlp_g[l].reshape(1, D), w1_b[l], w2_b[l],
                      fg=final_norm_g.reshape(1, D) if l == L - 1 else None)
    return xf.reshape(B, S, D)
```

```python
import functools

import jax
import jax.numpy as jnp
from jax import lax
from jax.experimental import pallas as pl
from jax.experimental.pallas import tpu as pltpu

f32 = jnp.float32
bf16 = jnp.bfloat16

D_MODEL = 1024
GLA_HEADS = 4
GLA_DK = 64
GLA_DV = 128
GLA_RANK = 16
GLA_TAU = 16.0
GLA_CHUNK = 64
SGU_GROUPS = 4
SGU_DC = 128
SGU_CHUNK = 128
D_FF = 4 * D_MODEL
EPS = 1e-6
LOG2E = 1.4426950408889634

GLA_QK = GLA_HEADS * GLA_DK
GLA_V = GLA_HEADS * GLA_DV
SGU_W = SGU_GROUPS * SGU_DC
LANES = 128
A_PAD = LANES
HEAD_PAIRS = GLA_HEADS // 2
PAIR_K = 2 * GLA_DK
PAIR_V = 2 * GLA_DV

_PROJ_COLS = (("q", GLA_QK), ("k", GLA_QK), ("v", GLA_V), ("g", GLA_V),
              ("su", SGU_W), ("sv", SGU_W), ("a", A_PAD))
D_PROJ = sum(w for _, w in _PROJ_COLS)

VMEM_LIMIT_BYTES = 56 * 1024 * 1024

PROJ_TM = 1024
GLA_TS = 512
CUMSUM_TILE = 256
MLP_TM = 512
FF_CHUNK = 1024


def _rms(x, g):
    return x * lax.rsqrt(jnp.mean(x * x, axis=-1, keepdims=True) + EPS) * g


def _gelu(x):
    return 0.5 * x * (1.0 + lax.erf(x * (2.0 ** -0.5)))


def _dot(a, b):
    return jnp.dot(a, b, preferred_element_type=f32)


def _proj_kernel(x_ref, g_ref, w_ref, *out_refs):
    h = _rms(x_ref[...], g_ref[...]).astype(bf16)
    lo = 0
    for ref, (_, width) in zip(out_refs, _PROJ_COLS):
        ref[...] = _dot(h, w_ref[:, lo:lo + width]).astype(ref.dtype)
        lo += width


def _proj(x, g, w):
    T = x.shape[0]
    tm = min(PROJ_TM, T)
    row = lambda width: pl.BlockSpec((tm, width), lambda i: (i, 0))
    const = lambda shape: pl.BlockSpec(shape, lambda i: (0, 0))
    return pl.pallas_call(
        _proj_kernel,
        grid=(T // tm,),
        in_specs=[row(D_MODEL), const((1, D_MODEL)), const((D_MODEL, D_PROJ))],
        out_specs=[row(width) for _, width in _PROJ_COLS],
        out_shape=[jax.ShapeDtypeStruct((T, width), bf16) for _, width in _PROJ_COLS],
        compiler_params=pltpu.CompilerParams(
            dimension_semantics=("arbitrary",), vmem_limit_bytes=VMEM_LIMIT_BYTES),
        name="proj",
    )(x, g, w)


def _log2_sigmoid(x):
    return jnp.minimum(x, 0.0) * LOG2E - jnp.log2(1.0 + jnp.exp2(jnp.abs(x) * (-LOG2E)))


class _GlaScan:
    def __init__(self, reverse, q_ref, k_ref, v_ref, a_ref, wa_ref, ba_ref, s_ref, out_ref):
        self.reverse = reverse
        self.refs = (q_ref, k_ref, v_ref, a_ref, wa_ref, ba_ref, s_ref, out_ref)
        self.nc = q_ref.shape[0] // GLA_CHUNK
        self.order = range(self.nc - 1, -1, -1) if reverse else range(self.nc)

    def prep(self):
        q_ref, k_ref, _, a_ref, wa_ref, ba_ref, s_ref, _ = self.refs
        C, NC, reverse = GLA_CHUNK, self.nc, self.reverse
        TS = NC * C
        pre = _dot(a_ref[...], wa_ref[...]) + ba_ref[...]
        log_a = _log2_sigmoid(pre) * (1.0 / GLA_TAU)
        ct = min(CUMSUM_TILE, TS)
        r = lax.broadcasted_iota(jnp.int32, (ct, ct), 0)
        c = lax.broadcasted_iota(jnp.int32, (ct, ct), 1)
        tri = (c >= r) if reverse else (c <= r)
        l_bd = jnp.where(((r // C) == (c // C)) & tri, 1.0, 0.0).astype(bf16)
        hi = log_a.astype(bf16)
        lo = (log_a - hi.astype(f32)).astype(bf16)
        b = jnp.concatenate(
            [_dot(l_bd, hi[i:i + ct]) + _dot(l_bd, lo[i:i + ct]) for i in range(0, TS, ct)], axis=0)
        b = b.reshape(NC, C, GLA_QK)
        mid, last = (C - 1 - C // 2, 0) if reverse else (C // 2, C - 1)
        b_mid = b[:, mid:mid + 1, :]
        b_last = b[:, last:last + 1, :]
        q = q_ref[...].astype(f32).reshape(NC, C, GLA_QK) * (GLA_DK ** -0.5)
        k = k_ref[...].astype(f32).reshape(NC, C, GLA_QK)
        self.q_in = (q * jnp.exp2(b - b_mid)).astype(bf16)
        self.k_in = (k * jnp.exp2(b_mid - b)).astype(bf16)
        self.k_dec = (k * jnp.exp2(b_last - b)).astype(bf16)
        self.q_st = (q * jnp.exp2(b)).astype(bf16)
        self.dec = jnp.exp2(b_last)
        self.state = [[s_ref[p, 0], s_ref[p, 1]] for p in range(HEAD_PAIRS)]

    def scores(self):
        C = GLA_CHUNK
        lane = lax.broadcasted_iota(jnp.int32, (C, PAIR_K), 1)
        row = lax.broadcasted_iota(jnp.int32, (C, PAIR_K), 0)
        even_head = lane < GLA_DK
        j = lane % GLA_DK
        causal = (j > row) if self.reverse else (j <= row)
        self.sc = {}
        for n in self.order:
            for p in range(HEAD_PAIRS):
                ks = slice(p * PAIR_K, (p + 1) * PAIR_K)
                kin = self.k_in[n][:, ks]
                k_sep = jnp.concatenate(
                    [jnp.where(even_head, kin, 0), jnp.where(even_head, 0, kin)], axis=0)
                sc = lax.dot_general(self.q_in[n][:, ks], k_sep, (((1,), (1,)), ((), ())),
                                     preferred_element_type=f32)
                self.sc[n, p] = jnp.where(causal, sc, 0.0).astype(bf16)

    def chunk_kv(self):
        v_ref = self.refs[2]
        C = GLA_CHUNK
        self.kv = {}
        for n in self.order:
            for p in range(HEAD_PAIRS):
                ks = slice(p * PAIR_K, (p + 1) * PAIR_K)
                vp = v_ref[n * C:(n + 1) * C, p * PAIR_V:(p + 1) * PAIR_V]
                kv = lax.dot_general(self.k_dec[n][:, ks], vp, (((0,), (0,)), ((), ())),
                                     preferred_element_type=f32)
                dcol = jnp.broadcast_to(self.dec[n][:, ks], (PAIR_K, PAIR_K)).T
                self.kv[n, p] = (kv[:GLA_DK, :GLA_DV], kv[GLA_DK:, GLA_DV:], dcol[:GLA_DK], dcol[GLA_DK:])

    def emit(self, i):
        v_ref, out_ref = self.refs[2], self.refs[7]
        C = GLA_CHUNK
        n = self.order[i]
        zero_v = jnp.zeros((C, GLA_DV), bf16)
        rows = slice(n * C, (n + 1) * C)
        for p in range(HEAD_PAIRS):
            ks = slice(p * PAIR_K, (p + 1) * PAIR_K)
            vp = v_ref[rows, p * PAIR_V:(p + 1) * PAIR_V]
            s_even, s_odd = self.state[p]
            lhs = jnp.concatenate([self.sc[n, p], self.q_st[n][:, ks]], axis=1)
            rhs = jnp.concatenate(
                [jnp.concatenate([vp[:, :GLA_DV], zero_v], axis=1),
                 jnp.concatenate([zero_v, vp[:, GLA_DV:]], axis=1),
                 jnp.concatenate([s_even.astype(bf16), zero_v], axis=1),
                 jnp.concatenate([zero_v, s_odd.astype(bf16)], axis=1)], axis=0)
            out_ref[rows, p * PAIR_V:(p + 1) * PAIR_V] = _dot(lhs, rhs).astype(out_ref.dtype)
            kv_even, kv_odd, d_even, d_odd = self.kv[n, p]
            self.state[p] = [s_even * d_even + kv_even, s_odd * d_odd + kv_odd]

    def finish(self):
        s_ref = self.refs[6]
        for p in range(HEAD_PAIRS):
            s_ref[p, 0] = self.state[p][0]
            s_ref[p, 1] = self.state[p][1]


def _gla_kernel(qf_ref, kf_ref, vf_ref, af_ref, qb_ref, kb_ref, vb_ref, ab_ref,
                waf_ref, baf_ref, wab_ref, bab_ref, of_ref, ob_ref, s_ref):
    @pl.when(pl.program_id(1) == 0)
    def _():
        s_ref[...] = jnp.zeros_like(s_ref)

    scans = (_GlaScan(False, qf_ref, kf_ref, vf_ref, af_ref, waf_ref, baf_ref, s_ref.at[0], of_ref),
             _GlaScan(True, qb_ref, kb_ref, vb_ref, ab_ref, wab_ref, bab_ref, s_ref.at[1], ob_ref))
    for stage in ("prep", "scores", "chunk_kv"):
        for scan in scans:
            getattr(scan, stage)()
    for i in range(scans[0].nc):
        for scan in scans:
            scan.emit(i)
    for scan in scans:
        scan.finish()


def _gla(B, S, q, k, v, a, wa_f, ba_f, wa_b, ba_b):
    T = q.shape[0]
    ts = min(GLA_TS, S)
    nt = S // ts
    fwd = lambda width: pl.BlockSpec((ts, width), lambda b, t: (b * nt + t, 0))
    bwd = lambda width: pl.BlockSpec((ts, width), lambda b, t: (b * nt + (nt - 1 - t), 0))
    const = lambda shape: pl.BlockSpec(shape, lambda b, t: (0, 0))
    tiles = lambda spec: [spec(GLA_QK), spec(GLA_QK), spec(GLA_V), spec(A_PAD)]
    return pl.pallas_call(
        _gla_kernel,
        grid=(B, nt),
        in_specs=tiles(fwd) + tiles(bwd) + [const((A_PAD, GLA_QK)), const((1, GLA_QK))] * 2,
        out_specs=[fwd(GLA_V), bwd(GLA_V)],
        out_shape=[jax.ShapeDtypeStruct((T, GLA_V), bf16)] * 2,
        scratch_shapes=[pltpu.VMEM((2, HEAD_PAIRS, 2, GLA_DK, GLA_DV), f32)],
        compiler_params=pltpu.CompilerParams(
            dimension_semantics=("arbitrary", "arbitrary"), vmem_limit_bytes=VMEM_LIMIT_BYTES),
        name="gla",
    )(q, k, v, a, q, k, v, a, wa_f, ba_f, wa_b, ba_b)


def _mix_mlp_kernel(final, x_ref, of_ref, ob_ref, gate_ref, su_ref, sv_ref, ng_ref, sg_ref, ws_ref,
                    bs_ref, wo_ref, nm_ref, w1_ref, w2_ref, *rest):
    if final:
        fg_ref, out_ref = rest
    else:
        (out_ref,) = rest
    TM = x_ref.shape[0]
    P = SGU_CHUNK
    o = of_ref[...].astype(f32) + ob_ref[...].astype(f32)
    gate = gate_ref[...].astype(f32)
    o_gla = []
    for h in range(GLA_HEADS):
        hs = slice(h * GLA_DV, (h + 1) * GLA_DV)
        gh = gate[:, hs]
        o_gla.append((_rms(o[:, hs], ng_ref[:, hs]) * (gh * jax.nn.sigmoid(gh))).astype(bf16))
    o_gla = jnp.concatenate(o_gla, axis=1)
    su = _gelu(su_ref[...].astype(f32))
    sv = _gelu(sv_ref[...].astype(f32))
    mixed = []
    for g in range(SGU_GROUPS):
        gs = slice(g * SGU_DC, (g + 1) * SGU_DC)
        vn = _rms(sv[:, gs], sg_ref[:, gs]).astype(bf16)
        mixed.append(jnp.concatenate(
            [_dot(ws_ref[g], vn[m:m + P]) for m in range(0, TM, P)], axis=0))
    mixed = jnp.concatenate(mixed, axis=1).reshape(TM // P, P, SGU_W) + bs_ref[...]
    o_sgu = (su * mixed.reshape(TM, SGU_W)).astype(bf16)
    out_ref[...] = x_ref[...] + _dot(o_gla, wo_ref[:GLA_V, :]) + _dot(o_sgu, wo_ref[GLA_V:, :])
    x1 = out_ref[...]
    h = _rms(x1, nm_ref[...]).astype(bf16)
    mlp = None
    for f in range(0, D_FF, FF_CHUNK):
        u = jnp.maximum(_dot(h, w1_ref[:, f:f + FF_CHUNK]), 0.0)
        t = _dot((u * u).astype(bf16), w2_ref[f:f + FF_CHUNK, :])
        mlp = t if mlp is None else mlp + t
    x2 = x1 + mlp
    out_ref[...] = _rms(x2, fg_ref[...]) if final else x2


def _mix_mlp(x, o_f, o_b, gate, su, sv, ng, sg, ws, bs, wo, nm, w1, w2, fg=None):
    T = x.shape[0]
    tm = min(MLP_TM, T)
    final = fg is not None
    row = lambda width: pl.BlockSpec((tm, width), lambda i: (i, 0))

    def const(shape):
        return pl.BlockSpec(shape, lambda i: (0,) * len(shape), pipeline_mode=pl.Buffered(1))

    in_specs = [row(D_MODEL), row(GLA_V), row(GLA_V), row(GLA_V), row(SGU_W), row(SGU_W),
                const((1, GLA_V)), const((1, SGU_W)), const((SGU_GROUPS, SGU_CHUNK, SGU_CHUNK)),
                const((SGU_CHUNK, SGU_W)), const((D_MODEL, D_MODEL)), const((1, D_MODEL)),
                const((D_MODEL, D_FF)), const((D_FF, D_MODEL))]
    args = [x, o_f, o_b, gate, su, sv, ng, sg, ws, bs, wo, nm, w1, w2]
    if final:
        in_specs.append(const((1, D_MODEL)))
        args.append(fg)
    return pl.pallas_call(
        functools.partial(_mix_mlp_kernel, final),
        grid=(T // tm,),
        in_specs=in_specs,
        out_specs=row(D_MODEL),
        out_shape=jax.ShapeDtypeStruct((T, D_MODEL), f32),
        compiler_params=pltpu.CompilerParams(
            dimension_semantics=("arbitrary",), vmem_limit_bytes=VMEM_LIMIT_BYTES),
        name="mix_mlp_final" if final else "mix_mlp",
    )(*args)


def _permute_w_in(w_in):
    L = w_in.shape[0]
    q, k, v, g, a_f, a_b, su, sv = jnp.split(
        w_in, (GLA_QK, 2 * GLA_QK, 2 * GLA_QK + GLA_V, 2 * GLA_QK + 2 * GLA_V,
               2 * GLA_QK + 2 * GLA_V + GLA_RANK, 2 * GLA_QK + 2 * GLA_V + 2 * GLA_RANK,
               2 * GLA_QK + 2 * GLA_V + 2 * GLA_RANK + SGU_W), axis=-1)
    pad = jnp.zeros((L, D_MODEL, A_PAD - 2 * GLA_RANK), w_in.dtype)
    return jnp.concatenate([q, k, v, g, su, sv, a_f, a_b, pad], axis=-1).astype(bf16)


def _pad_decay_w(w_a2, offset):
    L = w_a2.shape[0]
    out = jnp.zeros((L, A_PAD, GLA_QK), w_a2.dtype)
    return out.at[:, offset:offset + GLA_RANK, :].set(w_a2).astype(bf16)


def kernel(x, norm_mix_g, w_in, w_a2_fwd, b_a_fwd, w_a2_bwd, b_a_bwd, gla_norm_g, sgu_norm_g,
           w_s, b_s, w_out, norm_mlp_g, w_mlp1, w_mlp2, final_norm_g):
    B, S, D = x.shape
    L = w_in.shape[0]
    assert D == D_MODEL and S % max(GLA_CHUNK, SGU_CHUNK) == 0
    T = B * S

    w_in_p = _permute_w_in(w_in)
    wa_f = _pad_decay_w(w_a2_fwd, 0)
    wa_b = _pad_decay_w(w_a2_bwd, GLA_RANK)
    ws_b = w_s.astype(bf16)
    bs_full = jnp.repeat(jnp.swapaxes(b_s, 1, 2), SGU_DC, axis=2)
    wo_b = w_out.astype(bf16)
    w1_b = w_mlp1.astype(bf16)
    w2_b = w_mlp2.astype(bf16)

    xf = x.reshape(T, D)
    for l in range(L):
        q, k, v, g, su, sv, a = _proj(xf, norm_mix_g[l].reshape(1, D), w_in_p[l])
        o_f, o_b = _gla(B, S, q, k, v, a, wa_f[l], b_a_fwd[l].reshape(1, GLA_QK),
                        wa_b[l], b_a_bwd[l].reshape(1, GLA_QK))
        xf = _mix_mlp(xf, o_f, o_b, g, su, sv, gla_norm_g[l].reshape(1, GLA_V),
                      sgu_norm_g[l].reshape(1, SGU_W), ws_b[l], bs_full[l],
                      wo_b[l], norm_mlp_g[l].reshape(1, D), w1_b[l], w2_b[l],
                      fg=final_norm_g.reshape(1, D) if l == L - 1 else None)
    return xf.reshape(B, S, D)
```

```python
import functools

import jax
import jax.numpy as jnp
from jax import lax
from jax.experimental import pallas as pl
from jax.experimental.pallas import tpu as pltpu

f32 = jnp.float32
bf16 = jnp.bfloat16

D_MODEL = 1024
GLA_HEADS = 4
GLA_DK = 64
GLA_DV = 128
GLA_RANK = 16
GLA_TAU = 16.0
GLA_CHUNK = 64
SGU_GROUPS = 4
SGU_DC = 128
SGU_CHUNK = 128
D_FF = 4 * D_MODEL
EPS = 1e-6
LOG2E = 1.4426950408889634

GLA_QK = GLA_HEADS * GLA_DK
GLA_V = GLA_HEADS * GLA_DV
SGU_W = SGU_GROUPS * SGU_DC
LANES = 128
A_PAD = LANES
HEAD_PAIRS = GLA_HEADS // 2
PAIR_K = 2 * GLA_DK
PAIR_V = 2 * GLA_DV

_PROJ_COLS = (("q", GLA_QK), ("k", GLA_QK), ("v", GLA_V), ("g", GLA_V),
              ("su", SGU_W), ("sv", SGU_W), ("a", A_PAD))
D_PROJ = sum(w for _, w in _PROJ_COLS)

VMEM_LIMIT_BYTES = 56 * 1024 * 1024

PROJ_TM = 1024
GLA_TS = 512
PREP_ROWS = 128
MLP_TM = 512
FF_CHUNK = 1024


def _rms(x, g):
    return x * lax.rsqrt(jnp.mean(x * x, axis=-1, keepdims=True) + EPS) * g


def _gelu(x):
    return 0.5 * x * (1.0 + lax.erf(x * (2.0 ** -0.5)))


def _dot(a, b):
    return jnp.dot(a, b, preferred_element_type=f32)


def _proj_kernel(x_ref, g_ref, w_ref, *out_refs):
    h = _rms(x_ref[...], g_ref[...]).astype(bf16)
    lo = 0
    for ref, (_, width) in zip(out_refs, _PROJ_COLS):
        ref[...] = _dot(h, w_ref[:, lo:lo + width]).astype(ref.dtype)
        lo += width


def _proj(l, x, g, w):
    T = x.shape[0]
    tm = min(PROJ_TM, T)
    row = lambda width: pl.BlockSpec((tm, width), lambda i: (i, 0))
    layer = lambda shape: pl.BlockSpec((None,) + shape, lambda i: (l, 0, 0))
    return pl.pallas_call(
        _proj_kernel,
        grid=(T // tm,),
        in_specs=[row(D_MODEL), layer((1, D_MODEL)), layer((D_MODEL, D_PROJ))],
        out_specs=[row(width) for _, width in _PROJ_COLS],
        out_shape=[jax.ShapeDtypeStruct((T, width), bf16) for _, width in _PROJ_COLS],
        compiler_params=pltpu.CompilerParams(
            dimension_semantics=("arbitrary",), vmem_limit_bytes=VMEM_LIMIT_BYTES),
        name="proj",
    )(x, g, w)


def _log2_sigmoid(x):
    return jnp.minimum(x, 0.0) * LOG2E - jnp.log2(1.0 + jnp.exp2(jnp.abs(x) * (-LOG2E)))


N_PREP = 4
SCORE_W, KV_W, EMIT_W = 2, 3, 4


class _GlaScan:
    def __init__(self, reverse, q_ref, k_ref, a_ref, v_ref, wa_ref, ba_ref, out_ref, s_ref,
                 o_scr, w_prep, w_dec, r_prep, r_dec):
        self.reverse = reverse
        self.q_ref, self.k_ref, self.a_ref, self.v_ref = q_ref, k_ref, a_ref, v_ref
        self.wa_ref, self.ba_ref, self.out_ref, self.s_ref = wa_ref, ba_ref, out_ref, s_ref
        self.o_scr = o_scr
        self.w_prep, self.w_dec, self.r_prep, self.r_dec = w_prep, w_dec, r_prep, r_dec
        self.nc = q_ref.shape[0] // GLA_CHUNK
        self.order = range(self.nc - 1, -1, -1) if reverse else range(self.nc)
        self.split, self.cum, self.sc, self.kv = {}, {}, {}, {}
        self.state = None

    def prep_a(self, j):
        rows = slice(j * PREP_ROWS, (j + 1) * PREP_ROWS)
        pre = _dot(self.a_ref[rows, :], self.wa_ref[...]) + self.ba_ref[...]
        log_a = _log2_sigmoid(pre) * (1.0 / GLA_TAU)
        hi = log_a.astype(bf16)
        lo = (log_a - hi.astype(f32)).astype(bf16)
        self.split[j] = (hi, lo)

    def prep_b(self, j):
        C, R = GLA_CHUNK, PREP_ROWS
        r = lax.broadcasted_iota(jnp.int32, (R, R), 0)
        c = lax.broadcasted_iota(jnp.int32, (R, R), 1)
        tri = (c >= r) if self.reverse else (c <= r)
        l_bd = jnp.where(((r // C) == (c // C)) & tri, 1.0, 0.0).astype(bf16)
        hi, lo = self.split.pop(j)
        self.cum[j] = _dot(l_bd, hi) + _dot(l_bd, lo)

    def prep_c(self, j):
        C, R = GLA_CHUNK, PREP_ROWS
        NC = R // C
        rows = slice(j * R, (j + 1) * R)
        b = self.cum.pop(j).reshape(NC, C, GLA_QK)
        mid, last = (C - 1 - C // 2, 0) if self.reverse else (C // 2, C - 1)
        b_mid = b[:, mid:mid + 1, :]
        b_last = b[:, last:last + 1, :]
        q = self.q_ref[rows, :].astype(f32).reshape(NC, C, GLA_QK) * (GLA_DK ** -0.5)
        k = self.k_ref[rows, :].astype(f32).reshape(NC, C, GLA_QK)
        for idx, val in enumerate((q * jnp.exp2(b - b_mid), k * jnp.exp2(b_mid - b),
                                   k * jnp.exp2(b_last - b), q * jnp.exp2(b))):
            self.w_prep[idx, rows, :] = val.reshape(R, GLA_QK).astype(bf16)
        self.w_dec[j * NC:(j + 1) * NC, :] = jnp.exp2(b_last).reshape(NC, GLA_QK)

    def load_state(self):
        self.state = [[self.s_ref[p, 0], self.s_ref[p, 1]] for p in range(HEAD_PAIRS)]

    def score_unit(self, n, p):
        C = GLA_CHUNK
        lane = lax.broadcasted_iota(jnp.int32, (C, PAIR_K), 1)
        row = lax.broadcasted_iota(jnp.int32, (C, PAIR_K), 0)
        even_head = lane < GLA_DK
        j = lane % GLA_DK
        causal = (j > row) if self.reverse else (j <= row)
        rows = slice(n * C, (n + 1) * C)
        ks = slice(p * PAIR_K, (p + 1) * PAIR_K)
        kin = self.r_prep[1, rows, ks]
        k_sep = jnp.concatenate(
            [jnp.where(even_head, kin, 0), jnp.where(even_head, 0, kin)], axis=0)
        sc = lax.dot_general(self.r_prep[0, rows, ks], k_sep, (((1,), (1,)), ((), ())),
                             preferred_element_type=f32)
        self.sc[n, p] = jnp.where(causal, sc, 0.0).astype(bf16)

    def kv_unit(self, n, p):
        C = GLA_CHUNK
        rows = slice(n * C, (n + 1) * C)
        ks = slice(p * PAIR_K, (p + 1) * PAIR_K)
        vp = self.v_ref[rows, p * PAIR_V:(p + 1) * PAIR_V]
        kv = lax.dot_general(self.r_prep[2, rows, ks], vp, (((0,), (0,)), ((), ())),
                             preferred_element_type=f32)
        dcol = jnp.broadcast_to(self.r_dec[n:n + 1, ks], (PAIR_K, PAIR_K)).T
        self.kv[n, p] = (kv[:GLA_DK, :GLA_DV], kv[GLA_DK:, GLA_DV:], dcol[:GLA_DK], dcol[GLA_DK:])
        zero_v = jnp.zeros((C, GLA_DV), bf16)
        v_bd = jnp.concatenate([jnp.concatenate([vp[:, :GLA_DV], zero_v], axis=1),
                                jnp.concatenate([zero_v, vp[:, GLA_DV:]], axis=1)], axis=0)
        self.o_scr[rows, p * PAIR_V:(p + 1) * PAIR_V] = _dot(self.sc.pop((n, p)), v_bd)

    def emit(self, i):
        C = GLA_CHUNK
        n = self.order[i]
        zero_v = jnp.zeros((GLA_DK, GLA_DV), bf16)
        rows = slice(n * C, (n + 1) * C)
        for p in range(HEAD_PAIRS):
            ks = slice(p * PAIR_K, (p + 1) * PAIR_K)
            vs = slice(p * PAIR_V, (p + 1) * PAIR_V)
            s_even, s_odd = self.state[p]
            s_bd = jnp.concatenate(
                [jnp.concatenate([s_even.astype(bf16), zero_v], axis=1),
                 jnp.concatenate([zero_v, s_odd.astype(bf16)], axis=1)], axis=0)
            o = self.o_scr[rows, vs] + _dot(self.r_prep[3, rows, ks], s_bd)
            self.out_ref[rows, vs] = o.astype(self.out_ref.dtype)
            kv_even, kv_odd, d_even, d_odd = self.kv.pop((n, p))
            self.state[p] = [s_even * d_even + kv_even, s_odd * d_odd + kv_odd]

    def store_state(self):
        for p in range(HEAD_PAIRS):
            self.s_ref[p, 0] = self.state[p][0]
            self.s_ref[p, 1] = self.state[p][1]


def _interleave(main, fill):
    total = sum(w for _, w in main)
    out, done, seen = [], 0, 0
    for thunk, w in main:
        want = min(len(fill), -(-(seen * len(fill)) // total))
        out += fill[done:want]
        done = max(done, want)
        out.append(thunk)
        seen += w
    return out + fill[done:]


def _gla_kernel(qf_ref, kf_ref, af_ref, qb_ref, kb_ref, ab_ref, vf_ref, vb_ref,
                waf_ref, baf_ref, wab_ref, bab_ref, of_ref, ob_ref,
                s_ref, o_scr, prep0, dec0, prep1, dec1):
    t = pl.program_id(1)

    @pl.when((pl.program_id(0) == 0) & (t == 0))
    def _():
        prep1[...] = jnp.zeros_like(prep1)
        dec1[...] = jnp.zeros_like(dec1)

    @pl.when(t <= 1)
    def _():
        s_ref[...] = jnp.zeros_like(s_ref)

    def body(w_prep, w_dec, r_prep, r_dec):
        scans = (_GlaScan(False, qf_ref, kf_ref, af_ref, vf_ref, waf_ref, baf_ref, of_ref,
                          s_ref.at[0], o_scr.at[0], w_prep.at[0], w_dec.at[0], r_prep.at[0], r_dec.at[0]),
                 _GlaScan(True, qb_ref, kb_ref, ab_ref, vb_ref, wab_ref, bab_ref, ob_ref,
                          s_ref.at[1], o_scr.at[1], w_prep.at[1], w_dec.at[1], r_prep.at[1], r_dec.at[1]))
        nc = scans[0].nc
        units = [(scan, scan.order[i], p) for i in range(nc) for scan in scans
                 for p in range(HEAD_PAIRS)]
        main = [(scan.load_state, 0) for scan in scans]
        main += [(functools.partial(scan.score_unit, n, p), SCORE_W) for scan, n, p in units]
        main += [(functools.partial(scan.kv_unit, n, p), KV_W) for scan, n, p in units]
        main += [(functools.partial(scan.emit, i), EMIT_W) for i in range(nc) for scan in scans]
        pieces = qf_ref.shape[0] // PREP_ROWS
        fill = []
        for step in range(pieces + 2):
            for stage, lag in (("prep_a", 0), ("prep_b", 1), ("prep_c", 2)):
                if 0 <= step - lag < pieces:
                    fill += [functools.partial(getattr(scan, stage), step - lag) for scan in scans]
        for thunk in _interleave(main, fill):
            thunk()
        for scan in scans:
            scan.store_state()

    @pl.when(t % 2 == 0)
    def _():
        body(prep0, dec0, prep1, dec1)

    @pl.when(t % 2 == 1)
    def _():
        body(prep1, dec1, prep0, dec0)


def _gla(l, B, S, q, k, v, a, wa_f, ba_f, wa_b, ba_b):
    T = q.shape[0]
    ts = min(GLA_TS, S)
    nt = S // ts
    nc = ts // GLA_CHUNK
    ahead = lambda t: jnp.minimum(t, nt - 1)
    behind = lambda t: jnp.maximum(t - 1, 0)
    fwd = lambda width, tile: pl.BlockSpec((ts, width), lambda b, t: (b * nt + tile(t), 0))
    bwd = lambda width, tile: pl.BlockSpec((ts, width), lambda b, t: (b * nt + (nt - 1 - tile(t)), 0))
    layer = lambda shape: pl.BlockSpec((None,) + shape, lambda b, t: (l, 0, 0))
    prepared = lambda spec: [spec(GLA_QK, ahead), spec(GLA_QK, ahead), spec(A_PAD, ahead)]
    prep_buf = pltpu.VMEM((2, N_PREP, ts, GLA_QK), bf16)
    dec_buf = pltpu.VMEM((2, nc, GLA_QK), f32)
    return pl.pallas_call(
        _gla_kernel,
        grid=(B, nt + 1),
        in_specs=(prepared(fwd) + prepared(bwd) + [fwd(GLA_V, behind), bwd(GLA_V, behind)]
                  + [layer((A_PAD, GLA_QK)), layer((1, GLA_QK))] * 2),
        out_specs=[fwd(GLA_V, behind), bwd(GLA_V, behind)],
        out_shape=[jax.ShapeDtypeStruct((T, GLA_V), bf16)] * 2,
        scratch_shapes=[pltpu.VMEM((2, HEAD_PAIRS, 2, GLA_DK, GLA_DV), f32),
                        pltpu.VMEM((2, ts, GLA_V), f32), prep_buf, dec_buf, prep_buf, dec_buf],
        compiler_params=pltpu.CompilerParams(
            dimension_semantics=("arbitrary", "arbitrary"), vmem_limit_bytes=VMEM_LIMIT_BYTES),
        name="gla",
    )(q, k, a, q, k, a, v, v, wa_f, ba_f, wa_b, ba_b)


def _mix_mlp_kernel(final, x_ref, of_ref, ob_ref, gate_ref, su_ref, sv_ref, ng_ref, sg_ref,
                    ws_ref, bs_ref, wo_ref, nm_ref, w1_ref, w2_ref, *rest):
    if final:
        fg_ref, out_ref = rest
    else:
        (out_ref,) = rest
    TM = x_ref.shape[0]
    P = SGU_CHUNK
    o = of_ref[...].astype(f32) + ob_ref[...].astype(f32)
    gate = gate_ref[...].astype(f32)
    o_gla = []
    for h in range(GLA_HEADS):
        hs = slice(h * GLA_DV, (h + 1) * GLA_DV)
        gh = gate[:, hs]
        o_gla.append((_rms(o[:, hs], ng_ref[:, hs]) * (gh * jax.nn.sigmoid(gh))).astype(bf16))
    o_gla = jnp.concatenate(o_gla, axis=1)
    su = _gelu(su_ref[...].astype(f32))
    sv = _gelu(sv_ref[...].astype(f32))
    mixed = []
    for g in range(SGU_GROUPS):
        gs = slice(g * SGU_DC, (g + 1) * SGU_DC)
        vn = _rms(sv[:, gs], sg_ref[:, gs]).astype(bf16)
        mixed.append(jnp.concatenate(
            [_dot(ws_ref[g], vn[m:m + P]) for m in range(0, TM, P)], axis=0))
    mixed = jnp.concatenate(mixed, axis=1).reshape(TM // P, P, SGU_W) + bs_ref[...]
    o_sgu = (su * mixed.reshape(TM, SGU_W)).astype(bf16)
    out_ref[...] = x_ref[...] + _dot(o_gla, wo_ref[:GLA_V, :]) + _dot(o_sgu, wo_ref[GLA_V:, :])
    x1 = out_ref[...]
    h = _rms(x1, nm_ref[...]).astype(bf16)
    mlp = None
    for f in range(0, D_FF, FF_CHUNK):
        u = jnp.maximum(_dot(h, w1_ref[:, f:f + FF_CHUNK]), 0.0)
        t = _dot((u * u).astype(bf16), w2_ref[f:f + FF_CHUNK, :])
        mlp = t if mlp is None else mlp + t
    x2 = x1 + mlp
    out_ref[...] = _rms(x2, fg_ref[...]) if final else x2


def _mix_mlp(l, x, o_f, o_b, gate, su, sv, ng, sg, ws, bs, wo, nm, w1, w2, fg=None):
    T = x.shape[0]
    tm = min(MLP_TM, T)
    final = fg is not None
    row = lambda width: pl.BlockSpec((tm, width), lambda i: (i, 0))

    def layer(shape):
        return pl.BlockSpec((None,) + shape, lambda i: (l,) + (0,) * len(shape),
                            pipeline_mode=pl.Buffered(1))

    in_specs = [row(D_MODEL), row(GLA_V), row(GLA_V), row(GLA_V), row(SGU_W), row(SGU_W),
                layer((1, GLA_V)), layer((1, SGU_W)), layer((SGU_GROUPS, SGU_CHUNK, SGU_CHUNK)),
                layer((SGU_CHUNK, SGU_W)), layer((D_MODEL, D_MODEL)), layer((1, D_MODEL)),
                layer((D_MODEL, D_FF)), layer((D_FF, D_MODEL))]
    args = [x, o_f, o_b, gate, su, sv, ng, sg, ws, bs, wo, nm, w1, w2]
    if final:
        in_specs.append(pl.BlockSpec((1, D_MODEL), lambda i: (0, 0), pipeline_mode=pl.Buffered(1)))
        args.append(fg)
    return pl.pallas_call(
        functools.partial(_mix_mlp_kernel, final),
        grid=(T // tm,),
        in_specs=in_specs,
        out_specs=row(D_MODEL),
        out_shape=jax.ShapeDtypeStruct((T, D_MODEL), f32),
        compiler_params=pltpu.CompilerParams(
            dimension_semantics=("arbitrary",), vmem_limit_bytes=VMEM_LIMIT_BYTES),
        name="mix_mlp_final" if final else "mix_mlp",
    )(*args)


def _permute_w_in(w_in):
    L = w_in.shape[0]
    q, k, v, g, a_f, a_b, su, sv = jnp.split(
        w_in, (GLA_QK, 2 * GLA_QK, 2 * GLA_QK + GLA_V, 2 * GLA_QK + 2 * GLA_V,
               2 * GLA_QK + 2 * GLA_V + GLA_RANK, 2 * GLA_QK + 2 * GLA_V + 2 * GLA_RANK,
               2 * GLA_QK + 2 * GLA_V + 2 * GLA_RANK + SGU_W), axis=-1)
    pad = jnp.zeros((L, D_MODEL, A_PAD - 2 * GLA_RANK), w_in.dtype)
    return jnp.concatenate([q, k, v, g, su, sv, a_f, a_b, pad], axis=-1).astype(bf16)


def _pad_decay_w(w_a2, offset):
    L = w_a2.shape[0]
    out = jnp.zeros((L, A_PAD, GLA_QK), w_a2.dtype)
    return out.at[:, offset:offset + GLA_RANK, :].set(w_a2).astype(bf16)


def kernel(x, norm_mix_g, w_in, w_a2_fwd, b_a_fwd, w_a2_bwd, b_a_bwd, gla_norm_g, sgu_norm_g,
           w_s, b_s, w_out, norm_mlp_g, w_mlp1, w_mlp2, final_norm_g):
    B, S, D = x.shape
    L = w_in.shape[0]
    assert D == D_MODEL and S % max(GLA_CHUNK, SGU_CHUNK, PREP_ROWS) == 0
    T = B * S

    w_in_p = _permute_w_in(w_in)
    wa_f = _pad_decay_w(w_a2_fwd, 0)
    wa_b = _pad_decay_w(w_a2_bwd, GLA_RANK)
    ws_b = w_s.astype(bf16)
    bs_full = jnp.repeat(jnp.swapaxes(b_s, 1, 2), SGU_DC, axis=2)
    wo_b = w_out.astype(bf16)
    w1_b = w_mlp1.astype(bf16)
    w2_b = w_mlp2.astype(bf16)

    xf = x.reshape(T, D)
    for l in range(L):
        q, k, v, g, su, sv, a = _proj(l, xf, norm_mix_g.reshape(L, 1, D), w_in_p)
        o_f, o_b = _gla(l, B, S, q, k, v, a, wa_f, b_a_fwd.reshape(L, 1, GLA_QK),
                        wa_b, b_a_bwd.reshape(L, 1, GLA_QK))
        xf = _mix_mlp(l, xf, o_f, o_b, g, su, sv, gla_norm_g.reshape(L, 1, GLA_V),
                      sgu_norm_g.reshape(L, 1, SGU_W), ws_b, bs_full, wo_b,
                      norm_mlp_g.reshape(L, 1, D), w1_b, w2_b,
                      fg=final_norm_g.reshape(1, D) if l == L - 1 else None)
    return xf.reshape(B, S, D)
```

```python
import functools

import jax
import jax.numpy as jnp
from jax import lax
from jax.experimental import pallas as pl
from jax.experimental.pallas import tpu as pltpu

f32 = jnp.float32
bf16 = jnp.bfloat16

D_MODEL = 1024
GLA_HEADS = 4
GLA_DK = 64
GLA_DV = 128
GLA_RANK = 16
GLA_TAU = 16.0
GLA_CHUNK = 64
SGU_GROUPS = 4
SGU_DC = 128
SGU_CHUNK = 128
D_FF = 4 * D_MODEL
EPS = 1e-6
LOG2E = 1.4426950408889634

GLA_QK = GLA_HEADS * GLA_DK
GLA_V = GLA_HEADS * GLA_DV
SGU_W = SGU_GROUPS * SGU_DC
LANES = 128
A_PAD = LANES
HEAD_PAIRS = GLA_HEADS // 2
PAIR_K = 2 * GLA_DK
PAIR_V = 2 * GLA_DV

_PROJ_COLS = (("su", SGU_W), ("v", GLA_V), ("sv", SGU_W), ("q", GLA_QK), ("k", GLA_QK),
              ("g", GLA_V), ("a", A_PAD))
D_PROJ = sum(w for _, w in _PROJ_COLS)

VMEM_LIMIT_BYTES = 56 * 1024 * 1024

PROJ_TM = 1024
GLA_TS = 1024
PREP_ROWS = 128
MLP_TM = 1024
FF_CHUNK = 1024


def _rms(x, g):
    return x * lax.rsqrt(jnp.mean(x * x, axis=-1, keepdims=True) + EPS) * g


def _gelu(x):
    return 0.5 * x * (1.0 + lax.erf(x * (2.0 ** -0.5)))


def _dot(a, b):
    return jnp.dot(a, b, preferred_element_type=f32)


def _proj_kernel(x_ref, g_ref, w_ref, sg_ref, *out_refs):
    h = _rms(x_ref[...], g_ref[...]).astype(bf16)
    lo = 0
    for ref, (name, width) in zip(out_refs, _PROJ_COLS):
        z = _dot(h, w_ref[:, lo:lo + width])
        if name == "su":
            z = _gelu(z)
        elif name == "sv":
            z = _gelu(z)
            z = jnp.concatenate(
                [_rms(z[:, c:c + SGU_DC], sg_ref[:, c:c + SGU_DC]) for c in range(0, SGU_W, SGU_DC)],
                axis=1)
        elif name == "g":
            z = z * jax.nn.sigmoid(z)
        ref[...] = z.astype(ref.dtype)
        lo += width


def _proj(l, x, g, w, sg):
    T = x.shape[0]
    tm = min(PROJ_TM, T)
    row = lambda width: pl.BlockSpec((tm, width), lambda i: (i, 0))
    layer = lambda shape: pl.BlockSpec((None,) + shape, lambda i: (l, 0, 0))
    return pl.pallas_call(
        _proj_kernel,
        grid=(T // tm,),
        in_specs=[row(D_MODEL), layer((1, D_MODEL)), layer((D_MODEL, D_PROJ)), layer((1, SGU_W))],
        out_specs=[row(width) for _, width in _PROJ_COLS],
        out_shape=[jax.ShapeDtypeStruct((T, width), bf16) for _, width in _PROJ_COLS],
        compiler_params=pltpu.CompilerParams(
            dimension_semantics=("arbitrary",), vmem_limit_bytes=VMEM_LIMIT_BYTES),
        name="proj",
    )(x, g, w, sg)


def _log2_sigmoid(x):
    return jnp.minimum(x, 0.0) * LOG2E - jnp.log2(1.0 + jnp.exp2(jnp.abs(x) * (-LOG2E)))


N_PREP = 4
SCORE_W, KV_W, EMIT_W = 2, 3, 4


class _GlaScan:
    def __init__(self, reverse, q_ref, k_ref, a_ref, v_ref, wa_ref, ba_ref, out_ref, s_ref,
                 o_scr, w_prep, w_dec, r_prep, r_dec):
        self.reverse = reverse
        self.q_ref, self.k_ref, self.a_ref, self.v_ref = q_ref, k_ref, a_ref, v_ref
        self.wa_ref, self.ba_ref, self.out_ref, self.s_ref = wa_ref, ba_ref, out_ref, s_ref
        self.o_scr = o_scr
        self.w_prep, self.w_dec, self.r_prep, self.r_dec = w_prep, w_dec, r_prep, r_dec
        self.nc = q_ref.shape[0] // GLA_CHUNK
        self.order = range(self.nc - 1, -1, -1) if reverse else range(self.nc)
        self.split, self.cum, self.sc, self.kv = {}, {}, {}, {}
        self.state = None

    def prep_a(self, j):
        rows = slice(j * PREP_ROWS, (j + 1) * PREP_ROWS)
        pre = _dot(self.a_ref[rows, :], self.wa_ref[...]) + self.ba_ref[...]
        log_a = _log2_sigmoid(pre) * (1.0 / GLA_TAU)
        hi = log_a.astype(bf16)
        lo = (log_a - hi.astype(f32)).astype(bf16)
        self.split[j] = (hi, lo)

    def prep_b(self, j):
        C, R = GLA_CHUNK, PREP_ROWS
        r = lax.broadcasted_iota(jnp.int32, (R, R), 0)
        c = lax.broadcasted_iota(jnp.int32, (R, R), 1)
        tri = (c >= r) if self.reverse else (c <= r)
        l_bd = jnp.where(((r // C) == (c // C)) & tri, 1.0, 0.0).astype(bf16)
        hi, lo = self.split.pop(j)
        self.cum[j] = _dot(l_bd, hi) + _dot(l_bd, lo)

    def prep_c(self, j):
        C, R = GLA_CHUNK, PREP_ROWS
        NC = R // C
        rows = slice(j * R, (j + 1) * R)
        b = self.cum.pop(j).reshape(NC, C, GLA_QK)
        mid, last = (C - 1 - C // 2, 0) if self.reverse else (C // 2, C - 1)
        b_mid = b[:, mid:mid + 1, :]
        b_last = b[:, last:last + 1, :]
        q = self.q_ref[rows, :].astype(f32).reshape(NC, C, GLA_QK) * (GLA_DK ** -0.5)
        k = self.k_ref[rows, :].astype(f32).reshape(NC, C, GLA_QK)
        for idx, val in enumerate((q * jnp.exp2(b - b_mid), k * jnp.exp2(b_mid - b),
                                   k * jnp.exp2(b_last - b), q * jnp.exp2(b))):
            self.w_prep[idx, rows, :] = val.reshape(R, GLA_QK).astype(bf16)
        self.w_dec[j * NC:(j + 1) * NC, :] = jnp.exp2(b_last).reshape(NC, GLA_QK)

    def load_state(self):
        self.state = [[self.s_ref[p, 0], self.s_ref[p, 1]] for p in range(HEAD_PAIRS)]

    def score_unit(self, n, p):
        C = GLA_CHUNK
        lane = lax.broadcasted_iota(jnp.int32, (C, PAIR_K), 1)
        row = lax.broadcasted_iota(jnp.int32, (C, PAIR_K), 0)
        even_head = lane < GLA_DK
        j = lane % GLA_DK
        causal = (j > row) if self.reverse else (j <= row)
        rows = slice(n * C, (n + 1) * C)
        ks = slice(p * PAIR_K, (p + 1) * PAIR_K)
        kin = self.r_prep[1, rows, ks]
        k_sep = jnp.concatenate(
            [jnp.where(even_head, kin, 0), jnp.where(even_head, 0, kin)], axis=0)
        sc = lax.dot_general(self.r_prep[0, rows, ks], k_sep, (((1,), (1,)), ((), ())),
                             preferred_element_type=f32)
        self.sc[n, p] = jnp.where(causal, sc, 0.0).astype(bf16)

    def kv_unit(self, n, p):
        C = GLA_CHUNK
        rows = slice(n * C, (n + 1) * C)
        ks = slice(p * PAIR_K, (p + 1) * PAIR_K)
        vp = self.v_ref[rows, p * PAIR_V:(p + 1) * PAIR_V]
        kv = lax.dot_general(self.r_prep[2, rows, ks], vp, (((0,), (0,)), ((), ())),
                             preferred_element_type=f32)
        dcol = jnp.broadcast_to(self.r_dec[n:n + 1, ks], (PAIR_K, PAIR_K)).T
        self.kv[n, p] = (kv[:GLA_DK, :GLA_DV], kv[GLA_DK:, GLA_DV:], dcol[:GLA_DK], dcol[GLA_DK:])
        zero_v = jnp.zeros((C, GLA_DV), bf16)
        v_bd = jnp.concatenate([jnp.concatenate([vp[:, :GLA_DV], zero_v], axis=1),
                                jnp.concatenate([zero_v, vp[:, GLA_DV:]], axis=1)], axis=0)
        self.o_scr[rows, p * PAIR_V:(p + 1) * PAIR_V] = _dot(self.sc.pop((n, p)), v_bd)

    def emit(self, i):
        C = GLA_CHUNK
        n = self.order[i]
        zero_v = jnp.zeros((GLA_DK, GLA_DV), bf16)
        rows = slice(n * C, (n + 1) * C)
        for p in range(HEAD_PAIRS):
            ks = slice(p * PAIR_K, (p + 1) * PAIR_K)
            vs = slice(p * PAIR_V, (p + 1) * PAIR_V)
            s_even, s_odd = self.state[p]
            s_bd = jnp.concatenate(
                [jnp.concatenate([s_even.astype(bf16), zero_v], axis=1),
                 jnp.concatenate([zero_v, s_odd.astype(bf16)], axis=1)], axis=0)
            o = self.o_scr[rows, vs] + _dot(self.r_prep[3, rows, ks], s_bd)
            self.out_ref[rows, vs] = o.astype(self.out_ref.dtype)
            kv_even, kv_odd, d_even, d_odd = self.kv.pop((n, p))
            self.state[p] = [s_even * d_even + kv_even, s_odd * d_odd + kv_odd]

    def store_state(self):
        for p in range(HEAD_PAIRS):
            self.s_ref[p, 0] = self.state[p][0]
            self.s_ref[p, 1] = self.state[p][1]


def _interleave(main, fill):
    total = sum(w for _, w in main)
    out, done, seen = [], 0, 0
    for thunk, w in main:
        want = min(len(fill), -(-(seen * len(fill)) // total))
        out += fill[done:want]
        done = max(done, want)
        out.append(thunk)
        seen += w
    return out + fill[done:]


def _gla_kernel(nt, qf_ref, kf_ref, af_ref, qb_ref, kb_ref, ab_ref, vf_ref, vb_ref,
                waf_ref, baf_ref, wab_ref, bab_ref, of_ref, ob_ref,
                s_ref, o_scr, prep0, dec0, prep1, dec1):
    g = pl.program_id(0)

    @pl.when(g == 0)
    def _():
        prep1[...] = jnp.zeros_like(prep1)
        dec1[...] = jnp.zeros_like(dec1)

    @pl.when((g == 0) | ((g - 1) % nt == 0))
    def _():
        s_ref[...] = jnp.zeros_like(s_ref)

    def body(w_prep, w_dec, r_prep, r_dec):
        scans = (_GlaScan(False, qf_ref, kf_ref, af_ref, vf_ref, waf_ref, baf_ref, of_ref,
                          s_ref.at[0], o_scr.at[0], w_prep.at[0], w_dec.at[0], r_prep.at[0], r_dec.at[0]),
                 _GlaScan(True, qb_ref, kb_ref, ab_ref, vb_ref, wab_ref, bab_ref, ob_ref,
                          s_ref.at[1], o_scr.at[1], w_prep.at[1], w_dec.at[1], r_prep.at[1], r_dec.at[1]))
        nc = scans[0].nc
        units = [(scan, scan.order[i], p) for i in range(nc) for scan in scans
                 for p in range(HEAD_PAIRS)]
        main = [(scan.load_state, 0) for scan in scans]
        main += [(functools.partial(scan.score_unit, n, p), SCORE_W) for scan, n, p in units]
        main += [(functools.partial(scan.kv_unit, n, p), KV_W) for scan, n, p in units]
        main += [(functools.partial(scan.emit, i), EMIT_W) for i in range(nc) for scan in scans]
        pieces = qf_ref.shape[0] // PREP_ROWS
        fill = []
        for step in range(pieces + 2):
            for stage, lag in (("prep_a", 0), ("prep_b", 1), ("prep_c", 2)):
                if 0 <= step - lag < pieces:
                    fill += [functools.partial(getattr(scan, stage), step - lag) for scan in scans]
        for thunk in _interleave(main, fill):
            thunk()
        for scan in scans:
            scan.store_state()

    @pl.when(g % 2 == 0)
    def _():
        body(prep0, dec0, prep1, dec1)

    @pl.when(g % 2 == 1)
    def _():
        body(prep1, dec1, prep0, dec0)


def _gla(l, B, S, q, k, v, a, wa_f, ba_f, wa_b, ba_b):
    T = q.shape[0]
    ts = min(GLA_TS, S)
    nt = S // ts
    nc = ts // GLA_CHUNK
    n = B * nt
    ahead = lambda g: jnp.minimum(g, n - 1)
    behind = lambda g: jnp.maximum(g - 1, 0)
    fwd = lambda width, pair: pl.BlockSpec((ts, width), lambda g: (pair(g), 0))
    bwd = lambda width, pair: pl.BlockSpec(
        (ts, width), lambda g: (pair(g) // nt * nt + (nt - 1 - pair(g) % nt), 0))
    layer = lambda shape: pl.BlockSpec((None,) + shape, lambda g: (l, 0, 0))
    prepared = lambda spec: [spec(GLA_QK, ahead), spec(GLA_QK, ahead), spec(A_PAD, ahead)]
    prep_buf = pltpu.VMEM((2, N_PREP, ts, GLA_QK), bf16)
    dec_buf = pltpu.VMEM((2, nc, GLA_QK), f32)
    return pl.pallas_call(
        functools.partial(_gla_kernel, nt),
        grid=(n + 1,),
        in_specs=(prepared(fwd) + prepared(bwd) + [fwd(GLA_V, behind), bwd(GLA_V, behind)]
                  + [layer((A_PAD, GLA_QK)), layer((1, GLA_QK))] * 2),
        out_specs=[fwd(GLA_V, behind), bwd(GLA_V, behind)],
        out_shape=[jax.ShapeDtypeStruct((T, GLA_V), bf16)] * 2,
        scratch_shapes=[pltpu.VMEM((2, HEAD_PAIRS, 2, GLA_DK, GLA_DV), f32),
                        pltpu.VMEM((2, ts, GLA_V), f32), prep_buf, dec_buf, prep_buf, dec_buf],
        compiler_params=pltpu.CompilerParams(
            dimension_semantics=("arbitrary",), vmem_limit_bytes=VMEM_LIMIT_BYTES),
        name="gla",
    )(q, k, a, q, k, a, v, v, wa_f, ba_f, wa_b, ba_b)


def _mix_mlp_kernel(final, x_ref, of_ref, ob_ref, gate_ref, su_ref, vn_ref, ng_ref,
                    ws_ref, bs_ref, wo_ref, nm_ref, w1_ref, w2_ref, *rest):
    if final:
        fg_ref, out_ref = rest
    else:
        (out_ref,) = rest
    TM = x_ref.shape[0]
    P = SGU_CHUNK
    o = of_ref[...].astype(f32) + ob_ref[...].astype(f32)
    gate = gate_ref[...].astype(f32)
    o_gla = []
    for h in range(GLA_HEADS):
        hs = slice(h * GLA_DV, (h + 1) * GLA_DV)
        o_gla.append((_rms(o[:, hs], ng_ref[:, hs]) * gate[:, hs]).astype(bf16))
    o_gla = jnp.concatenate(o_gla, axis=1)
    su = su_ref[...].astype(f32)
    mixed = []
    for g in range(SGU_GROUPS):
        gs = slice(g * SGU_DC, (g + 1) * SGU_DC)
        mixed.append(jnp.concatenate(
            [_dot(ws_ref[g], vn_ref[m:m + P, gs]) for m in range(0, TM, P)], axis=0))
    mixed = jnp.concatenate(mixed, axis=1).reshape(TM // P, P, SGU_W) + bs_ref[...]
    o_sgu = (su * mixed.reshape(TM, SGU_W)).astype(bf16)
    out_ref[...] = x_ref[...] + _dot(o_gla, wo_ref[:GLA_V, :]) + _dot(o_sgu, wo_ref[GLA_V:, :])
    x1 = out_ref[...]
    h = _rms(x1, nm_ref[...]).astype(bf16)
    mlp = None
    for f in range(0, D_FF, FF_CHUNK):
        u = jnp.maximum(_dot(h, w1_ref[:, f:f + FF_CHUNK]), 0.0)
        t = _dot((u * u).astype(bf16), w2_ref[f:f + FF_CHUNK, :])
        mlp = t if mlp is None else mlp + t
    x2 = x1 + mlp
    out_ref[...] = _rms(x2, fg_ref[...]) if final else x2


def _mix_mlp(l, x, o_f, o_b, gate, su, vn, ng, ws, bs, wo, nm, w1, w2, fg=None):
    T = x.shape[0]
    tm = min(MLP_TM, T)
    final = fg is not None
    row = lambda width: pl.BlockSpec((tm, width), lambda i: (i, 0))

    def layer(shape):
        return pl.BlockSpec((None,) + shape, lambda i: (l,) + (0,) * len(shape),
                            pipeline_mode=pl.Buffered(1))

    in_specs = [row(D_MODEL), row(GLA_V), row(GLA_V), row(GLA_V), row(SGU_W), row(SGU_W),
                layer((1, GLA_V)), layer((SGU_GROUPS, SGU_CHUNK, SGU_CHUNK)),
                layer((SGU_CHUNK, SGU_W)), layer((D_MODEL, D_MODEL)), layer((1, D_MODEL)),
                layer((D_MODEL, D_FF)), layer((D_FF, D_MODEL))]
    args = [x, o_f, o_b, gate, su, vn, ng, ws, bs, wo, nm, w1, w2]
    if final:
        in_specs.append(pl.BlockSpec((1, D_MODEL), lambda i: (0, 0), pipeline_mode=pl.Buffered(1)))
        args.append(fg)
    return pl.pallas_call(
        functools.partial(_mix_mlp_kernel, final),
        grid=(T // tm,),
        in_specs=in_specs,
        out_specs=row(D_MODEL),
        out_shape=jax.ShapeDtypeStruct((T, D_MODEL), f32),
        compiler_params=pltpu.CompilerParams(
            dimension_semantics=("arbitrary",), vmem_limit_bytes=VMEM_LIMIT_BYTES),
        name="mix_mlp_final" if final else "mix_mlp",
    )(*args)


def _permute_w_in(w_in):
    L = w_in.shape[0]
    q, k, v, g, a_f, a_b, su, sv = jnp.split(
        w_in, (GLA_QK, 2 * GLA_QK, 2 * GLA_QK + GLA_V, 2 * GLA_QK + 2 * GLA_V,
               2 * GLA_QK + 2 * GLA_V + GLA_RANK, 2 * GLA_QK + 2 * GLA_V + 2 * GLA_RANK,
               2 * GLA_QK + 2 * GLA_V + 2 * GLA_RANK + SGU_W), axis=-1)
    pad = jnp.zeros((L, D_MODEL, A_PAD - 2 * GLA_RANK), w_in.dtype)
    return jnp.concatenate([su, v, sv, q, k, g, a_f, a_b, pad], axis=-1).astype(bf16)


def _pad_decay_w(w_a2, offset):
    L = w_a2.shape[0]
    out = jnp.zeros((L, A_PAD, GLA_QK), w_a2.dtype)
    return out.at[:, offset:offset + GLA_RANK, :].set(w_a2).astype(bf16)


def kernel(x, norm_mix_g, w_in, w_a2_fwd, b_a_fwd, w_a2_bwd, b_a_bwd, gla_norm_g, sgu_norm_g,
           w_s, b_s, w_out, norm_mlp_g, w_mlp1, w_mlp2, final_norm_g):
    B, S, D = x.shape
    L = w_in.shape[0]
    assert D == D_MODEL and S % max(GLA_CHUNK, SGU_CHUNK, PREP_ROWS) == 0
    T = B * S

    w_in_p = _permute_w_in(w_in)
    wa_f = _pad_decay_w(w_a2_fwd, 0)
    wa_b = _pad_decay_w(w_a2_bwd, GLA_RANK)
    ws_b = w_s.astype(bf16)
    bs_full = jnp.repeat(jnp.swapaxes(b_s, 1, 2), SGU_DC, axis=2)
    wo_b = w_out.astype(bf16)
    w1_b = w_mlp1.astype(bf16)
    w2_b = w_mlp2.astype(bf16)

    xf = x.reshape(T, D)
    for l in range(L):
        su, v, vn, q, k, gate, a = _proj(l, xf, norm_mix_g.reshape(L, 1, D), w_in_p,
                                         sgu_norm_g.reshape(L, 1, SGU_W))
        o_f, o_b = _gla(l, B, S, q, k, v, a, wa_f, b_a_fwd.reshape(L, 1, GLA_QK),
                        wa_b, b_a_bwd.reshape(L, 1, GLA_QK))
        xf = _mix_mlp(l, xf, o_f, o_b, gate, su, vn, gla_norm_g.reshape(L, 1, GLA_V), ws_b, bs_full, wo_b,
                      norm_mlp_g.reshape(L, 1, D), w1_b, w2_b,
                      fg=final_norm_g.reshape(1, D) if l == L - 1 else None)
    return xf.reshape(B, S, D)
```

```python
import functools

import jax
import jax.numpy as jnp
from jax import lax
from jax.experimental import pallas as pl
from jax.experimental.pallas import tpu as pltpu

f32 = jnp.float32
bf16 = jnp.bfloat16

D_MODEL = 1024
GLA_HEADS = 4
GLA_DK = 64
GLA_DV = 128
GLA_RANK = 16
GLA_TAU = 16.0
GLA_CHUNK = 64
SGU_GROUPS = 4
SGU_DC = 128
SGU_CHUNK = 128
D_FF = 4 * D_MODEL
EPS = 1e-6
LOG2E = 1.4426950408889634

GLA_QK = GLA_HEADS * GLA_DK
GLA_V = GLA_HEADS * GLA_DV
SGU_W = SGU_GROUPS * SGU_DC
LANES = 128
A_PAD = LANES
HEAD_PAIRS = GLA_HEADS // 2
PAIR_K = 2 * GLA_DK
PAIR_V = 2 * GLA_DV

_PROJ_COLS = (("su", SGU_W), ("v", GLA_V), ("sv", SGU_W), ("q", GLA_QK), ("k", GLA_QK),
              ("g", GLA_V), ("a", A_PAD))
D_PROJ = sum(w for _, w in _PROJ_COLS)

VMEM_LIMIT_BYTES = 56 * 1024 * 1024

PROJ_TM = 1024
GLA_TS = 1024
PREP_ROWS = 128
MLP_TM = 1024
MLP_SUB = 512
FF_CHUNK = 1024


def _rms(x, g):
    return x * lax.rsqrt(jnp.mean(x * x, axis=-1, keepdims=True) + EPS) * g


def _gelu(x):
    return 0.5 * x * (1.0 + lax.erf(x * (2.0 ** -0.5)))


def _dot(a, b):
    return jnp.dot(a, b, preferred_element_type=f32)


def _proj_kernel(x_ref, g_ref, w_ref, sg_ref, *out_refs):
    h = _rms(x_ref[...], g_ref[...]).astype(bf16)
    lo = 0
    for ref, (name, width) in zip(out_refs, _PROJ_COLS):
        z = _dot(h, w_ref[:, lo:lo + width])
        if name == "su":
            z = _gelu(z)
        elif name == "sv":
            z = _gelu(z)
            z = jnp.concatenate(
                [_rms(z[:, c:c + SGU_DC], sg_ref[:, c:c + SGU_DC]) for c in range(0, SGU_W, SGU_DC)],
                axis=1)
        elif name == "g":
            z = z * jax.nn.sigmoid(z)
        ref[...] = z.astype(ref.dtype)
        lo += width


def _proj(l, x, g, w, sg):
    T = x.shape[0]
    tm = min(PROJ_TM, T)
    row = lambda width: pl.BlockSpec((tm, width), lambda i: (i, 0))
    layer = lambda shape: pl.BlockSpec((None,) + shape, lambda i: (l, 0, 0))
    return pl.pallas_call(
        _proj_kernel,
        grid=(T // tm,),
        in_specs=[row(D_MODEL), layer((1, D_MODEL)), layer((D_MODEL, D_PROJ)), layer((1, SGU_W))],
        out_specs=[row(width) for _, width in _PROJ_COLS],
        out_shape=[jax.ShapeDtypeStruct((T, width), bf16) for _, width in _PROJ_COLS],
        compiler_params=pltpu.CompilerParams(
            dimension_semantics=("arbitrary",), vmem_limit_bytes=VMEM_LIMIT_BYTES),
        name="proj",
    )(x, g, w, sg)


def _log2_sigmoid(x):
    return jnp.minimum(x, 0.0) * LOG2E - jnp.log2(1.0 + jnp.exp2(jnp.abs(x) * (-LOG2E)))


N_PREP = 4
SCORE_W, KV_W, EMIT_W = 2, 3, 4


class _GlaScan:
    def __init__(self, reverse, q_ref, k_ref, a_ref, v_ref, wa_ref, ba_ref, out_ref, s_ref,
                 o_scr, w_prep, w_dec, r_prep, r_dec):
        self.reverse = reverse
        self.q_ref, self.k_ref, self.a_ref, self.v_ref = q_ref, k_ref, a_ref, v_ref
        self.wa_ref, self.ba_ref, self.out_ref, self.s_ref = wa_ref, ba_ref, out_ref, s_ref
        self.o_scr = o_scr
        self.w_prep, self.w_dec, self.r_prep, self.r_dec = w_prep, w_dec, r_prep, r_dec
        self.nc = q_ref.shape[0] // GLA_CHUNK
        self.order = range(self.nc - 1, -1, -1) if reverse else range(self.nc)
        self.split, self.cum, self.sc, self.kv = {}, {}, {}, {}
        self.state = None

    def prep_a(self, j):
        rows = slice(j * PREP_ROWS, (j + 1) * PREP_ROWS)
        pre = _dot(self.a_ref[rows, :], self.wa_ref[...]) + self.ba_ref[...]
        log_a = _log2_sigmoid(pre) * (1.0 / GLA_TAU)
        hi = log_a.astype(bf16)
        lo = (log_a - hi.astype(f32)).astype(bf16)
        self.split[j] = (hi, lo)

    def prep_b(self, j):
        C, R = GLA_CHUNK, PREP_ROWS
        r = lax.broadcasted_iota(jnp.int32, (R, R), 0)
        c = lax.broadcasted_iota(jnp.int32, (R, R), 1)
        tri = (c >= r) if self.reverse else (c <= r)
        l_bd = jnp.where(((r // C) == (c // C)) & tri, 1.0, 0.0).astype(bf16)
        hi, lo = self.split.pop(j)
        self.cum[j] = _dot(l_bd, hi) + _dot(l_bd, lo)

    def prep_c(self, j):
        C, R = GLA_CHUNK, PREP_ROWS
        NC = R // C
        rows = slice(j * R, (j + 1) * R)
        b = self.cum.pop(j).reshape(NC, C, GLA_QK)
        mid, last = (C - 1 - C // 2, 0) if self.reverse else (C // 2, C - 1)
        b_mid = b[:, mid:mid + 1, :]
        b_last = b[:, last:last + 1, :]
        q = self.q_ref[rows, :].astype(f32).reshape(NC, C, GLA_QK) * (GLA_DK ** -0.5)
        k = self.k_ref[rows, :].astype(f32).reshape(NC, C, GLA_QK)
        for idx, val in enumerate((q * jnp.exp2(b - b_mid), k * jnp.exp2(b_mid - b),
                                   k * jnp.exp2(b_last - b), q * jnp.exp2(b))):
            self.w_prep[idx, rows, :] = val.reshape(R, GLA_QK).astype(bf16)
        self.w_dec[j * NC:(j + 1) * NC, :] = jnp.exp2(b_last).reshape(NC, GLA_QK)

    def load_state(self):
        self.state = [[self.s_ref[p, 0], self.s_ref[p, 1]] for p in range(HEAD_PAIRS)]

    def score_unit(self, n, p):
        C = GLA_CHUNK
        lane = lax.broadcasted_iota(jnp.int32, (C, PAIR_K), 1)
        row = lax.broadcasted_iota(jnp.int32, (C, PAIR_K), 0)
        even_head = lane < GLA_DK
        j = lane % GLA_DK
        causal = (j > row) if self.reverse else (j <= row)
        rows = slice(n * C, (n + 1) * C)
        ks = slice(p * PAIR_K, (p + 1) * PAIR_K)
        kin = self.r_prep[1, rows, ks]
        k_sep = jnp.concatenate(
            [jnp.where(even_head, kin, 0), jnp.where(even_head, 0, kin)], axis=0)
        sc = lax.dot_general(self.r_prep[0, rows, ks], k_sep, (((1,), (1,)), ((), ())),
                             preferred_element_type=f32)
        self.sc[n, p] = jnp.where(causal, sc, 0.0).astype(bf16)

    def kv_unit(self, n, p):
        C = GLA_CHUNK
        rows = slice(n * C, (n + 1) * C)
        ks = slice(p * PAIR_K, (p + 1) * PAIR_K)
        vp = self.v_ref[rows, p * PAIR_V:(p + 1) * PAIR_V]
        kv = lax.dot_general(self.r_prep[2, rows, ks], vp, (((0,), (0,)), ((), ())),
                             preferred_element_type=f32)
        dcol = jnp.broadcast_to(self.r_dec[n:n + 1, ks], (PAIR_K, PAIR_K)).T
        self.kv[n, p] = (kv[:GLA_DK, :GLA_DV], kv[GLA_DK:, GLA_DV:], dcol[:GLA_DK], dcol[GLA_DK:])
        zero_v = jnp.zeros((C, GLA_DV), bf16)
        v_bd = jnp.concatenate([jnp.concatenate([vp[:, :GLA_DV], zero_v], axis=1),
                                jnp.concatenate([zero_v, vp[:, GLA_DV:]], axis=1)], axis=0)
        self.o_scr[rows, p * PAIR_V:(p + 1) * PAIR_V] = _dot(self.sc.pop((n, p)), v_bd)

    def emit(self, i):
        C = GLA_CHUNK
        n = self.order[i]
        zero_v = jnp.zeros((GLA_DK, GLA_DV), bf16)
        rows = slice(n * C, (n + 1) * C)
        for p in range(HEAD_PAIRS):
            ks = slice(p * PAIR_K, (p + 1) * PAIR_K)
            vs = slice(p * PAIR_V, (p + 1) * PAIR_V)
            s_even, s_odd = self.state[p]
            s_bd = jnp.concatenate(
                [jnp.concatenate([s_even.astype(bf16), zero_v], axis=1),
                 jnp.concatenate([zero_v, s_odd.astype(bf16)], axis=1)], axis=0)
            o = self.o_scr[rows, vs] + _dot(self.r_prep[3, rows, ks], s_bd)
            self.out_ref[rows, vs] = o.astype(self.out_ref.dtype)
            kv_even, kv_odd, d_even, d_odd = self.kv.pop((n, p))
            self.state[p] = [s_even * d_even + kv_even, s_odd * d_odd + kv_odd]

    def store_state(self):
        for p in range(HEAD_PAIRS):
            self.s_ref[p, 0] = self.state[p][0]
            self.s_ref[p, 1] = self.state[p][1]


def _interleave(main, fill):
    total = sum(w for _, w in main)
    out, done, seen = [], 0, 0
    for thunk, w in main:
        want = min(len(fill), -(-(seen * len(fill)) // total))
        out += fill[done:want]
        done = max(done, want)
        out.append(thunk)
        seen += w
    return out + fill[done:]


def _gla_kernel(nt, qf_ref, kf_ref, af_ref, qb_ref, kb_ref, ab_ref, vf_ref, vb_ref,
                waf_ref, baf_ref, wab_ref, bab_ref, of_ref, ob_ref,
                s_ref, o_scr, prep0, dec0, prep1, dec1):
    g = pl.program_id(0)

    @pl.when(g == 0)
    def _():
        prep1[...] = jnp.zeros_like(prep1)
        dec1[...] = jnp.zeros_like(dec1)

    @pl.when((g == 0) | ((g - 1) % nt == 0))
    def _():
        s_ref[...] = jnp.zeros_like(s_ref)

    def body(w_prep, w_dec, r_prep, r_dec):
        scans = (_GlaScan(False, qf_ref, kf_ref, af_ref, vf_ref, waf_ref, baf_ref, of_ref,
                          s_ref.at[0], o_scr.at[0], w_prep.at[0], w_dec.at[0], r_prep.at[0], r_dec.at[0]),
                 _GlaScan(True, qb_ref, kb_ref, ab_ref, vb_ref, wab_ref, bab_ref, ob_ref,
                          s_ref.at[1], o_scr.at[1], w_prep.at[1], w_dec.at[1], r_prep.at[1], r_dec.at[1]))
        nc = scans[0].nc
        main = [(scan.load_state, 0) for scan in scans]
        for i in range(nc + 2):
            for stage, lag, w in (("score_unit", 0, SCORE_W), ("kv_unit", 1, KV_W)):
                if 0 <= i - lag < nc:
                    main += [(functools.partial(getattr(scan, stage), scan.order[i - lag], p), w)
                             for scan in scans for p in range(HEAD_PAIRS)]
            if 0 <= i - 2 < nc:
                main += [(functools.partial(scan.emit, i - 2), EMIT_W) for scan in scans]
        pieces = qf_ref.shape[0] // PREP_ROWS
        fill = []
        for step in range(pieces + 2):
            for stage, lag in (("prep_a", 0), ("prep_b", 1), ("prep_c", 2)):
                if 0 <= step - lag < pieces:
                    fill += [functools.partial(getattr(scan, stage), step - lag) for scan in scans]
        for thunk in _interleave(main, fill):
            thunk()
        for scan in scans:
            scan.store_state()

    @pl.when(g % 2 == 0)
    def _():
        body(prep0, dec0, prep1, dec1)

    @pl.when(g % 2 == 1)
    def _():
        body(prep1, dec1, prep0, dec0)


def _gla(l, B, S, q, k, v, a, wa_f, ba_f, wa_b, ba_b):
    T = q.shape[0]
    ts = min(GLA_TS, S)
    nt = S // ts
    nc = ts // GLA_CHUNK
    n = B * nt
    ahead = lambda g: jnp.minimum(g, n - 1)
    behind = lambda g: jnp.maximum(g - 1, 0)
    fwd = lambda width, pair: pl.BlockSpec((ts, width), lambda g: (pair(g), 0))
    bwd = lambda width, pair: pl.BlockSpec(
        (ts, width), lambda g: (pair(g) // nt * nt + (nt - 1 - pair(g) % nt), 0))
    layer = lambda shape: pl.BlockSpec((None,) + shape, lambda g: (l, 0, 0))
    prepared = lambda spec: [spec(GLA_QK, ahead), spec(GLA_QK, ahead), spec(A_PAD, ahead)]
    prep_buf = pltpu.VMEM((2, N_PREP, ts, GLA_QK), bf16)
    dec_buf = pltpu.VMEM((2, nc, GLA_QK), f32)
    return pl.pallas_call(
        functools.partial(_gla_kernel, nt),
        grid=(n + 1,),
        in_specs=(prepared(fwd) + prepared(bwd) + [fwd(GLA_V, behind), bwd(GLA_V, behind)]
                  + [layer((A_PAD, GLA_QK)), layer((1, GLA_QK))] * 2),
        out_specs=[fwd(GLA_V, behind), bwd(GLA_V, behind)],
        out_shape=[jax.ShapeDtypeStruct((T, GLA_V), bf16)] * 2,
        scratch_shapes=[pltpu.VMEM((2, HEAD_PAIRS, 2, GLA_DK, GLA_DV), f32),
                        pltpu.VMEM((2, ts, GLA_V), f32), prep_buf, dec_buf, prep_buf, dec_buf],
        compiler_params=pltpu.CompilerParams(
            dimension_semantics=("arbitrary",), vmem_limit_bytes=VMEM_LIMIT_BYTES),
        name="gla",
    )(q, k, a, q, k, a, v, v, wa_f, ba_f, wa_b, ba_b)


def _mix_mlp_kernel(final, x_ref, of_ref, ob_ref, gate_ref, su_ref, vn_ref, ng_ref,
                    ws_ref, bs_ref, wo_ref, nm_ref, w1_ref, w2_ref, *rest):
    if final:
        fg_ref, out_ref = rest
    else:
        (out_ref,) = rest
    TM = x_ref.shape[0]
    P = SGU_CHUNK
    R = min(MLP_SUB, TM)

    def mix(rows):
        o = of_ref[rows, :].astype(f32) + ob_ref[rows, :].astype(f32)
        gate = gate_ref[rows, :].astype(f32)
        o_gla = jnp.concatenate(
            [(_rms(o[:, h * GLA_DV:(h + 1) * GLA_DV], ng_ref[:, h * GLA_DV:(h + 1) * GLA_DV])
              * gate[:, h * GLA_DV:(h + 1) * GLA_DV]).astype(bf16) for h in range(GLA_HEADS)], axis=1)
        su = su_ref[rows, :].astype(f32)
        mixed = jnp.concatenate(
            [jnp.concatenate([_dot(ws_ref[g], vn_ref[m:m + P, g * SGU_DC:(g + 1) * SGU_DC])
                              for m in range(rows.start, rows.stop, P)], axis=0)
             for g in range(SGU_GROUPS)], axis=1).reshape(R // P, P, SGU_W) + bs_ref[...]
        return o_gla, (su * mixed.reshape(R, SGU_W)).astype(bf16)

    subs = [slice(r, r + R) for r in range(0, TM, R)]
    mixes = {0: mix(subs[0])}
    for j, rows in enumerate(subs):
        o_gla, o_sgu = mixes.pop(j)
        out_ref[rows, :] = (x_ref[rows, :] + _dot(o_gla, wo_ref[:GLA_V, :])
                            + _dot(o_sgu, wo_ref[GLA_V:, :]))
        x1 = out_ref[rows, :]
        h = _rms(x1, nm_ref[...]).astype(bf16)
        mlp = None
        for f in range(0, D_FF, FF_CHUNK):
            u = jnp.maximum(_dot(h, w1_ref[:, f:f + FF_CHUNK]), 0.0)
            t = _dot((u * u).astype(bf16), w2_ref[f:f + FF_CHUNK, :])
            mlp = t if mlp is None else mlp + t
            if f == 0 and j + 1 < len(subs):
                mixes[j + 1] = mix(subs[j + 1])
        x2 = x1 + mlp
        out_ref[rows, :] = _rms(x2, fg_ref[...]) if final else x2


def _mix_mlp(l, x, o_f, o_b, gate, su, vn, ng, ws, bs, wo, nm, w1, w2, fg=None):
    T = x.shape[0]
    tm = min(MLP_TM, T)
    final = fg is not None
    row = lambda width: pl.BlockSpec((tm, width), lambda i: (i, 0))

    def layer(shape):
        return pl.BlockSpec((None,) + shape, lambda i: (l,) + (0,) * len(shape),
                            pipeline_mode=pl.Buffered(1))

    in_specs = [row(D_MODEL), row(GLA_V), row(GLA_V), row(GLA_V), row(SGU_W), row(SGU_W),
                layer((1, GLA_V)), layer((SGU_GROUPS, SGU_CHUNK, SGU_CHUNK)),
                layer((SGU_CHUNK, SGU_W)), layer((D_MODEL, D_MODEL)), layer((1, D_MODEL)),
                layer((D_MODEL, D_FF)), layer((D_FF, D_MODEL))]
    args = [x, o_f, o_b, gate, su, vn, ng, ws, bs, wo, nm, w1, w2]
    if final:
        in_specs.append(pl.BlockSpec((1, D_MODEL), lambda i: (0, 0), pipeline_mode=pl.Buffered(1)))
        args.append(fg)
    return pl.pallas_call(
        functools.partial(_mix_mlp_kernel, final),
        grid=(T // tm,),
        in_specs=in_specs,
        out_specs=row(D_MODEL),
        out_shape=jax.ShapeDtypeStruct((T, D_MODEL), f32),
        compiler_params=pltpu.CompilerParams(
            dimension_semantics=("arbitrary",), vmem_limit_bytes=VMEM_LIMIT_BYTES),
        name="mix_mlp_final" if final else "mix_mlp",
    )(*args)


def _permute_w_in(w_in):
    L = w_in.shape[0]
    q, k, v, g, a_f, a_b, su, sv = jnp.split(
        w_in, (GLA_QK, 2 * GLA_QK, 2 * GLA_QK + GLA_V, 2 * GLA_QK + 2 * GLA_V,
               2 * GLA_QK + 2 * GLA_V + GLA_RANK, 2 * GLA_QK + 2 * GLA_V + 2 * GLA_RANK,
               2 * GLA_QK + 2 * GLA_V + 2 * GLA_RANK + SGU_W), axis=-1)
    pad = jnp.zeros((L, D_MODEL, A_PAD - 2 * GLA_RANK), w_in.dtype)
    return jnp.concatenate([su, v, sv, q, k, g, a_f, a_b, pad], axis=-1).astype(bf16)


def _pad_decay_w(w_a2, offset):
    L = w_a2.shape[0]
    out = jnp.zeros((L, A_PAD, GLA_QK), w_a2.dtype)
    return out.at[:, offset:offset + GLA_RANK, :].set(w_a2).astype(bf16)


def kernel(x, norm_mix_g, w_in, w_a2_fwd, b_a_fwd, w_a2_bwd, b_a_bwd, gla_norm_g, sgu_norm_g,
           w_s, b_s, w_out, norm_mlp_g, w_mlp1, w_mlp2, final_norm_g):
    B, S, D = x.shape
    L = w_in.shape[0]
    assert D == D_MODEL and S % max(GLA_CHUNK, SGU_CHUNK, PREP_ROWS) == 0
    T = B * S

    w_in_p = _permute_w_in(w_in)
    wa_f = _pad_decay_w(w_a2_fwd, 0)
    wa_b = _pad_decay_w(w_a2_bwd, GLA_RANK)
    ws_b = w_s.astype(bf16)
    bs_full = jnp.repeat(jnp.swapaxes(b_s, 1, 2), SGU_DC, axis=2)
    wo_b = w_out.astype(bf16)
    w1_b = w_mlp1.astype(bf16)
    w2_b = w_mlp2.astype(bf16)

    xf = x.reshape(T, D)
    for l in range(L):
        su, v, vn, q, k, gate, a = _proj(l, xf, norm_mix_g.reshape(L, 1, D), w_in_p,
                                         sgu_norm_g.reshape(L, 1, SGU_W))
        o_f, o_b = _gla(l, B, S, q, k, v, a, wa_f, b_a_fwd.reshape(L, 1, GLA_QK),
                        wa_b, b_a_bwd.reshape(L, 1, GLA_QK))
        xf = _mix_mlp(l, xf, o_f, o_b, gate, su, vn, gla_norm_g.reshape(L, 1, GLA_V), ws_b, bs_full, wo_b,
                      norm_mlp_g.reshape(L, 1, D), w1_b, w2_b,
                      fg=final_norm_g.reshape(1, D) if l == L - 1 else None)
    return xf.reshape(B, S, D)
```

```python
import functools

import jax
import jax.numpy as jnp
from jax import lax
from jax.experimental import pallas as pl
from jax.experimental.pallas import tpu as pltpu

f32 = jnp.float32
bf16 = jnp.bfloat16

D_MODEL = 1024
GLA_HEADS = 4
GLA_DK = 64
GLA_DV = 128
GLA_RANK = 16
GLA_TAU = 16.0
GLA_CHUNK = 64
SGU_GROUPS = 4
SGU_DC = 128
SGU_CHUNK = 128
D_FF = 4 * D_MODEL
EPS = 1e-6
LOG2E = 1.4426950408889634

GLA_QK = GLA_HEADS * GLA_DK
GLA_V = GLA_HEADS * GLA_DV
SGU_W = SGU_GROUPS * SGU_DC
LANES = 128
A_PAD = LANES
HEAD_PAIRS = GLA_HEADS // 2
PAIR_K = 2 * GLA_DK
PAIR_V = 2 * GLA_DV

_PROJ_COLS = (("su", SGU_W), ("v", GLA_V), ("sv", SGU_W), ("q", GLA_QK), ("k", GLA_QK),
              ("g", GLA_V), ("a", A_PAD))
D_PROJ = sum(w for _, w in _PROJ_COLS)

VMEM_LIMIT_BYTES = 56 * 1024 * 1024

PROJ_TM = 1024
GLA_TS = 1024
PREP_ROWS = 128
MLP_TM = 1024
MLP_SUB = 512
FF_CHUNK = 1024


def _rms(x, g):
    return x * lax.rsqrt(jnp.mean(x * x, axis=-1, keepdims=True) + EPS) * g


def _gelu(x):
    return 0.5 * x * (1.0 + lax.erf(x * (2.0 ** -0.5)))


def _dot(a, b):
    return jnp.dot(a, b, preferred_element_type=f32)


def _proj_kernel(x_ref, g_ref, w_ref, sg_ref, *out_refs):
    h = _rms(x_ref[...], g_ref[...]).astype(bf16)
    lo = 0
    for ref, (name, width) in zip(out_refs, _PROJ_COLS):
        z = _dot(h, w_ref[:, lo:lo + width])
        if name == "su":
            z = _gelu(z)
        elif name == "sv":
            z = _gelu(z)
            z = jnp.concatenate(
                [_rms(z[:, c:c + SGU_DC], sg_ref[:, c:c + SGU_DC]) for c in range(0, SGU_W, SGU_DC)],
                axis=1)
        elif name == "g":
            z = z * jax.nn.sigmoid(z)
        ref[...] = z.astype(ref.dtype)
        lo += width


def _proj(l, x, g, w, sg):
    T = x.shape[0]
    tm = min(PROJ_TM, T)
    row = lambda width: pl.BlockSpec((tm, width), lambda i: (i, 0))
    layer = lambda shape: pl.BlockSpec((None,) + shape, lambda i: (l, 0, 0))
    return pl.pallas_call(
        _proj_kernel,
        grid=(T // tm,),
        in_specs=[row(D_MODEL), layer((1, D_MODEL)), layer((D_MODEL, D_PROJ)), layer((1, SGU_W))],
        out_specs=[row(width) for _, width in _PROJ_COLS],
        out_shape=[jax.ShapeDtypeStruct((T, width), bf16) for _, width in _PROJ_COLS],
        compiler_params=pltpu.CompilerParams(
            dimension_semantics=("arbitrary",), vmem_limit_bytes=VMEM_LIMIT_BYTES),
        name="proj",
    )(x, g, w, sg)


def _log2_sigmoid(x):
    return jnp.minimum(x, 0.0) * LOG2E - jnp.log2(1.0 + jnp.exp2(jnp.abs(x) * (-LOG2E)))


N_PREP = 4
SCORE_W, KV_W, EMIT_W = 2, 3, 4


class _GlaScan:
    def __init__(self, reverse, q_ref, k_ref, a_ref, v_ref, wa_ref, ba_ref, out_ref, s_ref,
                 w_prep, w_dec, r_prep, r_dec):
        self.reverse = reverse
        self.q_ref, self.k_ref, self.a_ref, self.v_ref = q_ref, k_ref, a_ref, v_ref
        self.wa_ref, self.ba_ref, self.out_ref, self.s_ref = wa_ref, ba_ref, out_ref, s_ref
        self.w_prep, self.w_dec, self.r_prep, self.r_dec = w_prep, w_dec, r_prep, r_dec
        self.nc = q_ref.shape[0] // GLA_CHUNK
        self.order = range(self.nc - 1, -1, -1) if reverse else range(self.nc)
        self.split, self.cum, self.sc, self.kv = {}, {}, {}, {}
        self.state = None

    def prep_a(self, j):
        rows = slice(j * PREP_ROWS, (j + 1) * PREP_ROWS)
        pre = _dot(self.a_ref[rows, :], self.wa_ref[...]) + self.ba_ref[...]
        log_a = _log2_sigmoid(pre) * (1.0 / GLA_TAU)
        hi = log_a.astype(bf16)
        lo = (log_a - hi.astype(f32)).astype(bf16)
        self.split[j] = (hi, lo)

    def prep_b(self, j):
        C, R = GLA_CHUNK, PREP_ROWS
        r = lax.broadcasted_iota(jnp.int32, (R, R), 0)
        c = lax.broadcasted_iota(jnp.int32, (R, R), 1)
        tri = (c >= r) if self.reverse else (c <= r)
        l_bd = jnp.where(((r // C) == (c // C)) & tri, 1.0, 0.0).astype(bf16)
        hi, lo = self.split.pop(j)
        self.cum[j] = _dot(jnp.concatenate([l_bd, l_bd], axis=1), jnp.concatenate([hi, lo], axis=0))

    def prep_c(self, j):
        C, R = GLA_CHUNK, PREP_ROWS
        NC = R // C
        rows = slice(j * R, (j + 1) * R)
        b = self.cum.pop(j).reshape(NC, C, GLA_QK)
        mid, last = (C - 1 - C // 2, 0) if self.reverse else (C // 2, C - 1)
        b_mid = b[:, mid:mid + 1, :]
        b_last = b[:, last:last + 1, :]
        q = self.q_ref[rows, :].astype(f32).reshape(NC, C, GLA_QK) * (GLA_DK ** -0.5)
        k = self.k_ref[rows, :].astype(f32).reshape(NC, C, GLA_QK)
        for idx, val in enumerate((q * jnp.exp2(b - b_mid), k * jnp.exp2(b_mid - b),
                                   k * jnp.exp2(b_last - b), q * jnp.exp2(b))):
            self.w_prep[idx, rows, :] = val.reshape(R, GLA_QK).astype(bf16)
        self.w_dec[j * NC:(j + 1) * NC, :] = jnp.exp2(b_last).reshape(NC, GLA_QK)

    def load_state(self):
        self.state = [[self.s_ref[p, 0], self.s_ref[p, 1]] for p in range(HEAD_PAIRS)]

    def score_unit(self, n, p):
        C = GLA_CHUNK
        lane = lax.broadcasted_iota(jnp.int32, (C, PAIR_K), 1)
        row = lax.broadcasted_iota(jnp.int32, (C, PAIR_K), 0)
        even_head = lane < GLA_DK
        j = lane % GLA_DK
        causal = (j > row) if self.reverse else (j <= row)
        rows = slice(n * C, (n + 1) * C)
        ks = slice(p * PAIR_K, (p + 1) * PAIR_K)
        kin = self.r_prep[1, rows, ks]
        k_sep = jnp.concatenate(
            [jnp.where(even_head, kin, 0), jnp.where(even_head, 0, kin)], axis=0)
        sc = lax.dot_general(self.r_prep[0, rows, ks], k_sep, (((1,), (1,)), ((), ())),
                             preferred_element_type=f32)
        self.sc[n, p] = jnp.where(causal, sc, 0.0).astype(bf16)

    def kv_unit(self, n, p):
        C = GLA_CHUNK
        rows = slice(n * C, (n + 1) * C)
        ks = slice(p * PAIR_K, (p + 1) * PAIR_K)
        vp = self.v_ref[rows, p * PAIR_V:(p + 1) * PAIR_V]
        kv = lax.dot_general(self.r_prep[2, rows, ks], vp, (((0,), (0,)), ((), ())),
                             preferred_element_type=f32)
        dcol = jnp.broadcast_to(self.r_dec[n:n + 1, ks], (PAIR_K, PAIR_K)).T
        self.kv[n, p] = (kv[:GLA_DK, :GLA_DV], kv[GLA_DK:, GLA_DV:], dcol[:GLA_DK], dcol[GLA_DK:])

    def emit(self, i):
        C = GLA_CHUNK
        n = self.order[i]
        zero_v = jnp.zeros((C, GLA_DV), bf16)
        rows = slice(n * C, (n + 1) * C)
        for p in range(HEAD_PAIRS):
            ks = slice(p * PAIR_K, (p + 1) * PAIR_K)
            vs = slice(p * PAIR_V, (p + 1) * PAIR_V)
            vp = self.v_ref[rows, vs]
            s_even, s_odd = self.state[p]
            lhs = jnp.concatenate([self.sc.pop((n, p)), self.r_prep[3, rows, ks]], axis=1)
            rhs = jnp.concatenate(
                [jnp.concatenate([vp[:, :GLA_DV], zero_v], axis=1),
                 jnp.concatenate([zero_v, vp[:, GLA_DV:]], axis=1),
                 jnp.concatenate([s_even.astype(bf16), zero_v], axis=1),
                 jnp.concatenate([zero_v, s_odd.astype(bf16)], axis=1)], axis=0)
            self.out_ref[rows, vs] = _dot(lhs, rhs).astype(self.out_ref.dtype)
            kv_even, kv_odd, d_even, d_odd = self.kv.pop((n, p))
            self.state[p] = [s_even * d_even + kv_even, s_odd * d_odd + kv_odd]

    def store_state(self):
        for p in range(HEAD_PAIRS):
            self.s_ref[p, 0] = self.state[p][0]
            self.s_ref[p, 1] = self.state[p][1]


def _interleave(main, fill):
    total = sum(w for _, w in main)
    out, done, seen = [], 0, 0
    for thunk, w in main:
        want = min(len(fill), -(-(seen * len(fill)) // total))
        out += fill[done:want]
        done = max(done, want)
        out.append(thunk)
        seen += w
    return out + fill[done:]


def _gla_kernel(nt, qf_ref, kf_ref, af_ref, qb_ref, kb_ref, ab_ref, vf_ref, vb_ref,
                waf_ref, baf_ref, wab_ref, bab_ref, of_ref, ob_ref,
                s_ref, prep0, dec0, prep1, dec1):
    g = pl.program_id(0)

    @pl.when(g == 0)
    def _():
        prep1[...] = jnp.zeros_like(prep1)
        dec1[...] = jnp.zeros_like(dec1)

    @pl.when((g == 0) | ((g - 1) % nt == 0))
    def _():
        s_ref[...] = jnp.zeros_like(s_ref)

    def body(w_prep, w_dec, r_prep, r_dec):
        scans = (_GlaScan(False, qf_ref, kf_ref, af_ref, vf_ref, waf_ref, baf_ref, of_ref,
                          s_ref.at[0], w_prep.at[0], w_dec.at[0], r_prep.at[0], r_dec.at[0]),
                 _GlaScan(True, qb_ref, kb_ref, ab_ref, vb_ref, wab_ref, bab_ref, ob_ref,
                          s_ref.at[1], w_prep.at[1], w_dec.at[1], r_prep.at[1], r_dec.at[1]))
        nc = scans[0].nc
        main = [(scan.load_state, 0) for scan in scans]
        for i in range(nc + 2):
            for stage, lag, w in (("score_unit", 0, SCORE_W), ("kv_unit", 1, KV_W)):
                if 0 <= i - lag < nc:
                    main += [(functools.partial(getattr(scan, stage), scan.order[i - lag], p), w)
                             for scan in scans for p in range(HEAD_PAIRS)]
            if 0 <= i - 2 < nc:
                main += [(functools.partial(scan.emit, i - 2), EMIT_W) for scan in scans]
        pieces = qf_ref.shape[0] // PREP_ROWS
        fill = []
        for step in range(pieces + 2):
            for stage, lag in (("prep_a", 0), ("prep_b", 1), ("prep_c", 2)):
                if 0 <= step - lag < pieces:
                    fill += [functools.partial(getattr(scan, stage), step - lag) for scan in scans]
        for thunk in _interleave(main, fill):
            thunk()
        for scan in scans:
            scan.store_state()

    @pl.when(g % 2 == 0)
    def _():
        body(prep0, dec0, prep1, dec1)

    @pl.when(g % 2 == 1)
    def _():
        body(prep1, dec1, prep0, dec0)


def _gla(l, B, S, q, k, v, a, wa_f, ba_f, wa_b, ba_b):
    T = q.shape[0]
    ts = min(GLA_TS, S)
    nt = S // ts
    nc = ts // GLA_CHUNK
    n = B * nt
    ahead = lambda g: jnp.minimum(g, n - 1)
    behind = lambda g: jnp.maximum(g - 1, 0)
    fwd = lambda width, pair: pl.BlockSpec((ts, width), lambda g: (pair(g), 0))
    bwd = lambda width, pair: pl.BlockSpec(
        (ts, width), lambda g: (pair(g) // nt * nt + (nt - 1 - pair(g) % nt), 0))
    layer = lambda shape: pl.BlockSpec((None,) + shape, lambda g: (l, 0, 0))
    prepared = lambda spec: [spec(GLA_QK, ahead), spec(GLA_QK, ahead), spec(A_PAD, ahead)]
    prep_buf = pltpu.VMEM((2, N_PREP, ts, GLA_QK), bf16)
    dec_buf = pltpu.VMEM((2, nc, GLA_QK), f32)
    return pl.pallas_call(
        functools.partial(_gla_kernel, nt),
        grid=(n + 1,),
        in_specs=(prepared(fwd) + prepared(bwd) + [fwd(GLA_V, behind), bwd(GLA_V, behind)]
                  + [layer((A_PAD, GLA_QK)), layer((1, GLA_QK))] * 2),
        out_specs=[fwd(GLA_V, behind), bwd(GLA_V, behind)],
        out_shape=[jax.ShapeDtypeStruct((T, GLA_V), bf16)] * 2,
        scratch_shapes=[pltpu.VMEM((2, HEAD_PAIRS, 2, GLA_DK, GLA_DV), f32),
                        prep_buf, dec_buf, prep_buf, dec_buf],
        compiler_params=pltpu.CompilerParams(
            dimension_semantics=("arbitrary",), vmem_limit_bytes=VMEM_LIMIT_BYTES),
        name="gla",
    )(q, k, a, q, k, a, v, v, wa_f, ba_f, wa_b, ba_b)


def _mix_mlp_kernel(final, x_ref, of_ref, ob_ref, gate_ref, su_ref, vn_ref, ng_ref,
                    ws_ref, bs_ref, wo_ref, nm_ref, w1_ref, w2_ref, *rest):
    if final:
        fg_ref, out_ref = rest
    else:
        (out_ref,) = rest
    TM = x_ref.shape[0]
    P = SGU_CHUNK
    R = min(MLP_SUB, TM)

    def mix(rows):
        o = of_ref[rows, :].astype(f32) + ob_ref[rows, :].astype(f32)
        gate = gate_ref[rows, :].astype(f32)
        o_gla = jnp.concatenate(
            [(_rms(o[:, h * GLA_DV:(h + 1) * GLA_DV], ng_ref[:, h * GLA_DV:(h + 1) * GLA_DV])
              * gate[:, h * GLA_DV:(h + 1) * GLA_DV]).astype(bf16) for h in range(GLA_HEADS)], axis=1)
        su = su_ref[rows, :].astype(f32)
        mixed = jnp.concatenate(
            [jnp.concatenate([_dot(ws_ref[g], vn_ref[m:m + P, g * SGU_DC:(g + 1) * SGU_DC])
                              for m in range(rows.start, rows.stop, P)], axis=0)
             for g in range(SGU_GROUPS)], axis=1).reshape(R // P, P, SGU_W) + bs_ref[...]
        return o_gla, (su * mixed.reshape(R, SGU_W)).astype(bf16)

    subs = [slice(r, r + R) for r in range(0, TM, R)]
    mixes = {0: mix(subs[0])}
    for j, rows in enumerate(subs):
        o_gla, o_sgu = mixes.pop(j)
        out_ref[rows, :] = (x_ref[rows, :] + _dot(o_gla, wo_ref[:GLA_V, :])
                            + _dot(o_sgu, wo_ref[GLA_V:, :]))
        x1 = out_ref[rows, :]
        h = _rms(x1, nm_ref[...]).astype(bf16)
        mlp = None
        for f in range(0, D_FF, FF_CHUNK):
            u = jnp.maximum(_dot(h, w1_ref[:, f:f + FF_CHUNK]), 0.0)
            t = _dot((u * u).astype(bf16), w2_ref[f:f + FF_CHUNK, :])
            mlp = t if mlp is None else mlp + t
            if f == 0 and j + 1 < len(subs):
                mixes[j + 1] = mix(subs[j + 1])
        x2 = x1 + mlp
        out_ref[rows, :] = _rms(x2, fg_ref[...]) if final else x2


def _mix_mlp(l, x, o_f, o_b, gate, su, vn, ng, ws, bs, wo, nm, w1, w2, fg=None):
    T = x.shape[0]
    tm = min(MLP_TM, T)
    final = fg is not None
    row = lambda width: pl.BlockSpec((tm, width), lambda i: (i, 0))

    def layer(shape):
        return pl.BlockSpec((None,) + shape, lambda i: (l,) + (0,) * len(shape),
                            pipeline_mode=pl.Buffered(1))

    in_specs = [row(D_MODEL), row(GLA_V), row(GLA_V), row(GLA_V), row(SGU_W), row(SGU_W),
                layer((1, GLA_V)), layer((SGU_GROUPS, SGU_CHUNK, SGU_CHUNK)),
                layer((SGU_CHUNK, SGU_W)), layer((D_MODEL, D_MODEL)), layer((1, D_MODEL)),
                layer((D_MODEL, D_FF)), layer((D_FF, D_MODEL))]
    args = [x, o_f, o_b, gate, su, vn, ng, ws, bs, wo, nm, w1, w2]
    if final:
        in_specs.append(pl.BlockSpec((1, D_MODEL), lambda i: (0, 0), pipeline_mode=pl.Buffered(1)))
        args.append(fg)
    return pl.pallas_call(
        functools.partial(_mix_mlp_kernel, final),
        grid=(T // tm,),
        in_specs=in_specs,
        out_specs=row(D_MODEL),
        out_shape=jax.ShapeDtypeStruct((T, D_MODEL), f32),
        compiler_params=pltpu.CompilerParams(
            dimension_semantics=("arbitrary",), vmem_limit_bytes=VMEM_LIMIT_BYTES),
        name="mix_mlp_final" if final else "mix_mlp",
    )(*args)


def _permute_w_in(w_in):
    L = w_in.shape[0]
    q, k, v, g, a_f, a_b, su, sv = jnp.split(
        w_in, (GLA_QK, 2 * GLA_QK, 2 * GLA_QK + GLA_V, 2 * GLA_QK + 2 * GLA_V,
               2 * GLA_QK + 2 * GLA_V + GLA_RANK, 2 * GLA_QK + 2 * GLA_V + 2 * GLA_RANK,
               2 * GLA_QK + 2 * GLA_V + 2 * GLA_RANK + SGU_W), axis=-1)
    pad = jnp.zeros((L, D_MODEL, A_PAD - 2 * GLA_RANK), w_in.dtype)
    return jnp.concatenate([su, v, sv, q, k, g, a_f, a_b, pad], axis=-1).astype(bf16)


def _pad_decay_w(w_a2, offset):
    L = w_a2.shape[0]
    out = jnp.zeros((L, A_PAD, GLA_QK), w_a2.dtype)
    return out.at[:, offset:offset + GLA_RANK, :].set(w_a2).astype(bf16)


def kernel(x, norm_mix_g, w_in, w_a2_fwd, b_a_fwd, w_a2_bwd, b_a_bwd, gla_norm_g, sgu_norm_g,
           w_s, b_s, w_out, norm_mlp_g, w_mlp1, w_mlp2, final_norm_g):
    B, S, D = x.shape
    L = w_in.shape[0]
    assert D == D_MODEL and S % max(GLA_CHUNK, SGU_CHUNK, PREP_ROWS) == 0
    T = B * S

    w_in_p = _permute_w_in(w_in)
    wa_f = _pad_decay_w(w_a2_fwd, 0)
    wa_b = _pad_decay_w(w_a2_bwd, GLA_RANK)
    ws_b = w_s.astype(bf16)
    bs_full = jnp.repeat(jnp.swapaxes(b_s, 1, 2), SGU_DC, axis=2)
    wo_b = w_out.astype(bf16)
    w1_b = w_mlp1.astype(bf16)
    w2_b = w_mlp2.astype(bf16)

    xf = x.reshape(T, D)
    for l in range(L):
        su, v, vn, q, k, gate, a = _proj(l, xf, norm_mix_g.reshape(L, 1, D), w_in_p,
                                         sgu_norm_g.reshape(L, 1, SGU_W))
        o_f, o_b = _gla(l, B, S, q, k, v, a, wa_f, b_a_fwd.reshape(L, 1, GLA_QK),
                        wa_b, b_a_bwd.reshape(L, 1, GLA_QK))
        xf = _mix_mlp(l, xf, o_f, o_b, gate, su, vn, gla_norm_g.reshape(L, 1, GLA_V), ws_b, bs_full, wo_b,
                      norm_mlp_g.reshape(L, 1, D), w1_b, w2_b,
                      fg=final_norm_g.reshape(1, D) if l == L - 1 else None)
    return xf.reshape(B, S, D)
```

```python
import functools

import jax
import jax.numpy as jnp
from jax import lax
from jax.experimental import pallas as pl
from jax.experimental.pallas import tpu as pltpu

f32 = jnp.float32
bf16 = jnp.bfloat16

D_MODEL = 1024
GLA_HEADS = 4
GLA_DK = 64
GLA_DV = 128
GLA_RANK = 16
GLA_TAU = 16.0
GLA_CHUNK = 64
SGU_GROUPS = 4
SGU_DC = 128
SGU_CHUNK = 128
D_FF = 4 * D_MODEL
EPS = 1e-6
LOG2E = 1.4426950408889634

GLA_QK = GLA_HEADS * GLA_DK
GLA_V = GLA_HEADS * GLA_DV
SGU_W = SGU_GROUPS * SGU_DC
LANES = 128
A_PAD = LANES
HEAD_PAIRS = GLA_HEADS // 2
PAIR_K = 2 * GLA_DK
PAIR_V = 2 * GLA_DV

_PROJ_COLS = (("su", SGU_W), ("v", GLA_V), ("sv", SGU_W), ("q", GLA_QK), ("k", GLA_QK),
              ("g", GLA_V), ("a", A_PAD))
D_PROJ = sum(w for _, w in _PROJ_COLS)

VMEM_LIMIT_BYTES = 56 * 1024 * 1024

PROJ_TM = 1024
PROJ_SUB = 512
GLA_TS = 1024
PREP_ROWS = 128
MLP_TM = 1024
MLP_SUB = 512
FF_CHUNK = 1024


def _rms(x, g):
    return x * lax.rsqrt(jnp.mean(x * x, axis=-1, keepdims=True) + EPS) * g


def _gelu(x):
    return 0.5 * x * (1.0 + lax.erf(x * (2.0 ** -0.5)))


def _dot(a, b):
    return jnp.dot(a, b, preferred_element_type=f32)


def _proj_kernel(x_ref, g_ref, w_ref, sg_ref, *out_refs):
    TM = x_ref.shape[0]
    R = min(PROJ_SUB, TM)
    subs = [slice(r, r + R) for r in range(0, TM, R)]
    normed = {0: _rms(x_ref[subs[0], :], g_ref[...]).astype(bf16)}
    for j, rows in enumerate(subs):
        h = normed.pop(j)
        lo = 0
        for ref, (name, width) in zip(out_refs, _PROJ_COLS):
            z = _dot(h, w_ref[:, lo:lo + width])
            if name == "su":
                z = _gelu(z)
            elif name == "sv":
                z = _gelu(z)
                z = jnp.concatenate(
                    [_rms(z[:, c:c + SGU_DC], sg_ref[:, c:c + SGU_DC])
                     for c in range(0, SGU_W, SGU_DC)], axis=1)
            elif name == "g":
                z = z * jax.nn.sigmoid(z)
            ref[rows, :] = z.astype(ref.dtype)
            if lo == 0 and j + 1 < len(subs):
                normed[j + 1] = _rms(x_ref[subs[j + 1], :], g_ref[...]).astype(bf16)
            lo += width


def _proj(l, x, g, w, sg):
    T = x.shape[0]
    tm = min(PROJ_TM, T)
    row = lambda width: pl.BlockSpec((tm, width), lambda i: (i, 0))
    layer = lambda shape: pl.BlockSpec((None,) + shape, lambda i: (l, 0, 0))
    return pl.pallas_call(
        _proj_kernel,
        grid=(T // tm,),
        in_specs=[row(D_MODEL), layer((1, D_MODEL)), layer((D_MODEL, D_PROJ)), layer((1, SGU_W))],
        out_specs=[row(width) for _, width in _PROJ_COLS],
        out_shape=[jax.ShapeDtypeStruct((T, width), bf16) for _, width in _PROJ_COLS],
        compiler_params=pltpu.CompilerParams(
            dimension_semantics=("arbitrary",), vmem_limit_bytes=VMEM_LIMIT_BYTES),
        name="proj",
    )(x, g, w, sg)


def _log2_sigmoid(x):
    return jnp.minimum(x, 0.0) * LOG2E - jnp.log2(1.0 + jnp.exp2(jnp.abs(x) * (-LOG2E)))


N_PREP = 4
SCORE_W, KV_W, EMIT_W = 2, 3, 4


class _GlaScan:
    def __init__(self, reverse, q_ref, k_ref, a_ref, v_ref, wa_ref, ba_ref, out_ref, s_ref,
                 o_scr, w_prep, w_dec, r_prep, r_dec):
        self.reverse = reverse
        self.q_ref, self.k_ref, self.a_ref, self.v_ref = q_ref, k_ref, a_ref, v_ref
        self.wa_ref, self.ba_ref, self.out_ref, self.s_ref = wa_ref, ba_ref, out_ref, s_ref
        self.o_scr = o_scr
        self.w_prep, self.w_dec, self.r_prep, self.r_dec = w_prep, w_dec, r_prep, r_dec
        self.nc = q_ref.shape[0] // GLA_CHUNK
        self.order = range(self.nc - 1, -1, -1) if reverse else range(self.nc)
        self.split, self.cum, self.sc, self.kv = {}, {}, {}, {}
        self.state = None

    def prep_a(self, j):
        rows = slice(j * PREP_ROWS, (j + 1) * PREP_ROWS)
        pre = _dot(self.a_ref[rows, :], self.wa_ref[...]) + self.ba_ref[...]
        log_a = _log2_sigmoid(pre) * (1.0 / GLA_TAU)
        hi = log_a.astype(bf16)
        lo = (log_a - hi.astype(f32)).astype(bf16)
        self.split[j] = (hi, lo)

    def prep_b(self, j):
        C, R = GLA_CHUNK, PREP_ROWS
        r = lax.broadcasted_iota(jnp.int32, (R, R), 0)
        c = lax.broadcasted_iota(jnp.int32, (R, R), 1)
        tri = (c >= r) if self.reverse else (c <= r)
        l_bd = jnp.where(((r // C) == (c // C)) & tri, 1.0, 0.0).astype(bf16)
        hi, lo = self.split.pop(j)
        self.cum[j] = _dot(l_bd, hi) + _dot(l_bd, lo)

    def prep_c(self, j):
        C, R = GLA_CHUNK, PREP_ROWS
        NC = R // C
        rows = slice(j * R, (j + 1) * R)
        b = self.cum.pop(j).reshape(NC, C, GLA_QK)
        mid, last = (C - 1 - C // 2, 0) if self.reverse else (C // 2, C - 1)
        b_mid = b[:, mid:mid + 1, :]
        b_last = b[:, last:last + 1, :]
        q = self.q_ref[rows, :].astype(f32).reshape(NC, C, GLA_QK) * (GLA_DK ** -0.5)
        k = self.k_ref[rows, :].astype(f32).reshape(NC, C, GLA_QK)
        for idx, val in enumerate((q * jnp.exp2(b - b_mid), k * jnp.exp2(b_mid - b),
                                   k * jnp.exp2(b_last - b), q * jnp.exp2(b))):
            self.w_prep[idx, rows, :] = val.reshape(R, GLA_QK).astype(bf16)
        self.w_dec[j * NC:(j + 1) * NC, :] = jnp.exp2(b_last).reshape(NC, GLA_QK)

    def load_state(self):
        self.state = [[self.s_ref[p, 0], self.s_ref[p, 1]] for p in range(HEAD_PAIRS)]

    def score_unit(self, n, p):
        C = GLA_CHUNK
        lane = lax.broadcasted_iota(jnp.int32, (C, PAIR_K), 1)
        row = lax.broadcasted_iota(jnp.int32, (C, PAIR_K), 0)
        even_head = lane < GLA_DK
        j = lane % GLA_DK
        causal = (j > row) if self.reverse else (j <= row)
        rows = slice(n * C, (n + 1) * C)
        ks = slice(p * PAIR_K, (p + 1) * PAIR_K)
        kin = self.r_prep[1, rows, ks]
        k_sep = jnp.concatenate(
            [jnp.where(even_head, kin, 0), jnp.where(even_head, 0, kin)], axis=0)
        sc = lax.dot_general(self.r_prep[0, rows, ks], k_sep, (((1,), (1,)), ((), ())),
                             preferred_element_type=f32)
        self.sc[n, p] = jnp.where(causal, sc, 0.0).astype(bf16)

    def kv_unit(self, n, p):
        C = GLA_CHUNK
        rows = slice(n * C, (n + 1) * C)
        ks = slice(p * PAIR_K, (p + 1) * PAIR_K)
        vp = self.v_ref[rows, p * PAIR_V:(p + 1) * PAIR_V]
        kv = lax.dot_general(self.r_prep[2, rows, ks], vp, (((0,), (0,)), ((), ())),
                             preferred_element_type=f32)
        dcol = jnp.broadcast_to(self.r_dec[n:n + 1, ks], (PAIR_K, PAIR_K)).T
        self.kv[n, p] = (kv[:GLA_DK, :GLA_DV], kv[GLA_DK:, GLA_DV:], dcol[:GLA_DK], dcol[GLA_DK:])
        zero_v = jnp.zeros((C, GLA_DV), bf16)
        v_bd = jnp.concatenate([jnp.concatenate([vp[:, :GLA_DV], zero_v], axis=1),
                                jnp.concatenate([zero_v, vp[:, GLA_DV:]], axis=1)], axis=0)
        self.o_scr[rows, p * PAIR_V:(p + 1) * PAIR_V] = _dot(self.sc.pop((n, p)), v_bd)

    def emit(self, i):
        C = GLA_CHUNK
        n = self.order[i]
        zero_v = jnp.zeros((GLA_DK, GLA_DV), bf16)
        rows = slice(n * C, (n + 1) * C)
        for p in range(HEAD_PAIRS):
            ks = slice(p * PAIR_K, (p + 1) * PAIR_K)
            vs = slice(p * PAIR_V, (p + 1) * PAIR_V)
            s_even, s_odd = self.state[p]
            s_bd = jnp.concatenate(
                [jnp.concatenate([s_even.astype(bf16), zero_v], axis=1),
                 jnp.concatenate([zero_v, s_odd.astype(bf16)], axis=1)], axis=0)
            o = self.o_scr[rows, vs] + _dot(self.r_prep[3, rows, ks], s_bd)
            self.out_ref[rows, vs] = o.astype(self.out_ref.dtype)
            kv_even, kv_odd, d_even, d_odd = self.kv.pop((n, p))
            self.state[p] = [s_even * d_even + kv_even, s_odd * d_odd + kv_odd]

    def store_state(self):
        for p in range(HEAD_PAIRS):
            self.s_ref[p, 0] = self.state[p][0]
            self.s_ref[p, 1] = self.state[p][1]


def _interleave(main, fill):
    total = sum(w for _, w in main)
    out, done, seen = [], 0, 0
    for thunk, w in main:
        want = min(len(fill), -(-(seen * len(fill)) // total))
        out += fill[done:want]
        done = max(done, want)
        out.append(thunk)
        seen += w
    return out + fill[done:]


def _gla_kernel(nt, qf_ref, kf_ref, af_ref, qb_ref, kb_ref, ab_ref, vf_ref, vb_ref,
                waf_ref, baf_ref, wab_ref, bab_ref, of_ref, ob_ref,
                s_ref, o_scr, prep0, dec0, prep1, dec1):
    g = pl.program_id(0)

    @pl.when(g == 0)
    def _():
        prep1[...] = jnp.zeros_like(prep1)
        dec1[...] = jnp.zeros_like(dec1)

    @pl.when((g == 0) | ((g - 1) % nt == 0))
    def _():
        s_ref[...] = jnp.zeros_like(s_ref)

    def body(w_prep, w_dec, r_prep, r_dec):
        scans = (_GlaScan(False, qf_ref, kf_ref, af_ref, vf_ref, waf_ref, baf_ref, of_ref,
                          s_ref.at[0], o_scr.at[0], w_prep.at[0], w_dec.at[0], r_prep.at[0], r_dec.at[0]),
                 _GlaScan(True, qb_ref, kb_ref, ab_ref, vb_ref, wab_ref, bab_ref, ob_ref,
                          s_ref.at[1], o_scr.at[1], w_prep.at[1], w_dec.at[1], r_prep.at[1], r_dec.at[1]))
        nc = scans[0].nc
        main = [(scan.load_state, 0) for scan in scans]
        for i in range(nc + 2):
            for stage, lag, w in (("score_unit", 0, SCORE_W), ("kv_unit", 1, KV_W)):
                if 0 <= i - lag < nc:
                    main += [(functools.partial(getattr(scan, stage), scan.order[i - lag], p), w)
                             for scan in scans for p in range(HEAD_PAIRS)]
            if 0 <= i - 2 < nc:
                main += [(functools.partial(scan.emit, i - 2), EMIT_W) for scan in scans]
        pieces = qf_ref.shape[0] // PREP_ROWS
        fill = []
        for step in range(pieces + 2):
            for stage, lag in (("prep_a", 0), ("prep_b", 1), ("prep_c", 2)):
                if 0 <= step - lag < pieces:
                    fill += [functools.partial(getattr(scan, stage), step - lag) for scan in scans]
        for thunk in _interleave(main, fill):
            thunk()
        for scan in scans:
            scan.store_state()

    @pl.when(g % 2 == 0)
    def _():
        body(prep0, dec0, prep1, dec1)

    @pl.when(g % 2 == 1)
    def _():
        body(prep1, dec1, prep0, dec0)


def _gla(l, B, S, q, k, v, a, wa_f, ba_f, wa_b, ba_b):
    T = q.shape[0]
    ts = min(GLA_TS, S)
    nt = S // ts
    nc = ts // GLA_CHUNK
    n = B * nt
    ahead = lambda g: jnp.minimum(g, n - 1)
    behind = lambda g: jnp.maximum(g - 1, 0)
    fwd = lambda width, pair: pl.BlockSpec((ts, width), lambda g: (pair(g), 0))
    bwd = lambda width, pair: pl.BlockSpec(
        (ts, width), lambda g: (pair(g) // nt * nt + (nt - 1 - pair(g) % nt), 0))
    layer = lambda shape: pl.BlockSpec((None,) + shape, lambda g: (l, 0, 0))
    prepared = lambda spec: [spec(GLA_QK, ahead), spec(GLA_QK, ahead), spec(A_PAD, ahead)]
    prep_buf = pltpu.VMEM((2, N_PREP, ts, GLA_QK), bf16)
    dec_buf = pltpu.VMEM((2, nc, GLA_QK), f32)
    return pl.pallas_call(
        functools.partial(_gla_kernel, nt),
        grid=(n + 1,),
        in_specs=(prepared(fwd) + prepared(bwd) + [fwd(GLA_V, behind), bwd(GLA_V, behind)]
                  + [layer((A_PAD, GLA_QK)), layer((1, GLA_QK))] * 2),
        out_specs=[fwd(GLA_V, behind), bwd(GLA_V, behind)],
        out_shape=[jax.ShapeDtypeStruct((T, GLA_V), bf16)] * 2,
        scratch_shapes=[pltpu.VMEM((2, HEAD_PAIRS, 2, GLA_DK, GLA_DV), f32),
                        pltpu.VMEM((2, ts, GLA_V), f32), prep_buf, dec_buf, prep_buf, dec_buf],
        compiler_params=pltpu.CompilerParams(
            dimension_semantics=("arbitrary",), vmem_limit_bytes=VMEM_LIMIT_BYTES),
        name="gla",
    )(q, k, a, q, k, a, v, v, wa_f, ba_f, wa_b, ba_b)


def _mix_mlp_kernel(final, x_ref, of_ref, ob_ref, gate_ref, su_ref, vn_ref, ng_ref,
                    ws_ref, bs_ref, wo_ref, nm_ref, w1_ref, w2_ref, *rest):
    if final:
        fg_ref, out_ref = rest
    else:
        (out_ref,) = rest
    TM = x_ref.shape[0]
    P = SGU_CHUNK
    R = min(MLP_SUB, TM)

    def mix(rows):
        o = of_ref[rows, :].astype(f32) + ob_ref[rows, :].astype(f32)
        gate = gate_ref[rows, :].astype(f32)
        o_gla = jnp.concatenate(
            [(_rms(o[:, h * GLA_DV:(h + 1) * GLA_DV], ng_ref[:, h * GLA_DV:(h + 1) * GLA_DV])
              * gate[:, h * GLA_DV:(h + 1) * GLA_DV]).astype(bf16) for h in range(GLA_HEADS)], axis=1)
        su = su_ref[rows, :].astype(f32)
        mixed = jnp.concatenate(
            [jnp.concatenate([_dot(ws_ref[g], vn_ref[m:m + P, g * SGU_DC:(g + 1) * SGU_DC])
                              for m in range(rows.start, rows.stop, P)], axis=0)
             for g in range(SGU_GROUPS)], axis=1).reshape(R // P, P, SGU_W) + bs_ref[...]
        return o_gla, (su * mixed.reshape(R, SGU_W)).astype(bf16)

    subs = [slice(r, r + R) for r in range(0, TM, R)]
    mixes = {0: mix(subs[0])}
    for j, rows in enumerate(subs):
        o_gla, o_sgu = mixes.pop(j)
        out_ref[rows, :] = (x_ref[rows, :] + _dot(o_gla, wo_ref[:GLA_V, :])
                            + _dot(o_sgu, wo_ref[GLA_V:, :]))
        x1 = out_ref[rows, :]
        h = _rms(x1, nm_ref[...]).astype(bf16)
        mlp = None
        for f in range(0, D_FF, FF_CHUNK):
            u = jnp.maximum(_dot(h, w1_ref[:, f:f + FF_CHUNK]), 0.0)
            t = _dot((u * u).astype(bf16), w2_ref[f:f + FF_CHUNK, :])
            mlp = t if mlp is None else mlp + t
            if f == 0 and j + 1 < len(subs):
                mixes[j + 1] = mix(subs[j + 1])
        x2 = x1 + mlp
        out_ref[rows, :] = _rms(x2, fg_ref[...]) if final else x2


def _mix_mlp(l, x, o_f, o_b, gate, su, vn, ng, ws, bs, wo, nm, w1, w2, fg=None):
    T = x.shape[0]
    tm = min(MLP_TM, T)
    final = fg is not None
    row = lambda width: pl.BlockSpec((tm, width), lambda i: (i, 0))

    def layer(shape):
        return pl.BlockSpec((None,) + shape, lambda i: (l,) + (0,) * len(shape),
                            pipeline_mode=pl.Buffered(1))

    in_specs = [row(D_MODEL), row(GLA_V), row(GLA_V), row(GLA_V), row(SGU_W), row(SGU_W),
                layer((1, GLA_V)), layer((SGU_GROUPS, SGU_CHUNK, SGU_CHUNK)),
                layer((SGU_CHUNK, SGU_W)), layer((D_MODEL, D_MODEL)), layer((1, D_MODEL)),
                layer((D_MODEL, D_FF)), layer((D_FF, D_MODEL))]
    args = [x, o_f, o_b, gate, su, vn, ng, ws, bs, wo, nm, w1, w2]
    if final:
        in_specs.append(pl.BlockSpec((1, D_MODEL), lambda i: (0, 0), pipeline_mode=pl.Buffered(1)))
        args.append(fg)
    return pl.pallas_call(
        functools.partial(_mix_mlp_kernel, final),
        grid=(T // tm,),
        in_specs=in_specs,
        out_specs=row(D_MODEL),
        out_shape=jax.ShapeDtypeStruct((T, D_MODEL), f32),
        compiler_params=pltpu.CompilerParams(
            dimension_semantics=("arbitrary",), vmem_limit_bytes=VMEM_LIMIT_BYTES),
        name="mix_mlp_final" if final else "mix_mlp",
    )(*args)


def _permute_w_in(w_in):
    L = w_in.shape[0]
    q, k, v, g, a_f, a_b, su, sv = jnp.split(
        w_in, (GLA_QK, 2 * GLA_QK, 2 * GLA_QK + GLA_V, 2 * GLA_QK + 2 * GLA_V,
               2 * GLA_QK + 2 * GLA_V + GLA_RANK, 2 * GLA_QK + 2 * GLA_V + 2 * GLA_RANK,
               2 * GLA_QK + 2 * GLA_V + 2 * GLA_RANK + SGU_W), axis=-1)
    pad = jnp.zeros((L, D_MODEL, A_PAD - 2 * GLA_RANK), w_in.dtype)
    return jnp.concatenate([su, v, sv, q, k, g, a_f, a_b, pad], axis=-1).astype(bf16)


def _pad_decay_w(w_a2, offset):
    L = w_a2.shape[0]
    out = jnp.zeros((L, A_PAD, GLA_QK), w_a2.dtype)
    return out.at[:, offset:offset + GLA_RANK, :].set(w_a2).astype(bf16)


def kernel(x, norm_mix_g, w_in, w_a2_fwd, b_a_fwd, w_a2_bwd, b_a_bwd, gla_norm_g, sgu_norm_g,
           w_s, b_s, w_out, norm_mlp_g, w_mlp1, w_mlp2, final_norm_g):
    B, S, D = x.shape
    L = w_in.shape[0]
    assert D == D_MODEL and S % max(GLA_CHUNK, SGU_CHUNK, PREP_ROWS) == 0
    T = B * S

    w_in_p = _permute_w_in(w_in)
    wa_f = _pad_decay_w(w_a2_fwd, 0)
    wa_b = _pad_decay_w(w_a2_bwd, GLA_RANK)
    ws_b = w_s.astype(bf16)
    bs_full = jnp.repeat(jnp.swapaxes(b_s, 1, 2), SGU_DC, axis=2)
    wo_b = w_out.astype(bf16)
    w1_b = w_mlp1.astype(bf16)
    w2_b = w_mlp2.astype(bf16)

    xf = x.reshape(T, D)
    for l in range(L):
        su, v, vn, q, k, gate, a = _proj(l, xf, norm_mix_g.reshape(L, 1, D), w_in_p,
                                         sgu_norm_g.reshape(L, 1, SGU_W))
        o_f, o_b = _gla(l, B, S, q, k, v, a, wa_f, b_a_fwd.reshape(L, 1, GLA_QK),
                        wa_b, b_a_bwd.reshape(L, 1, GLA_QK))
        xf = _mix_mlp(l, xf, o_f, o_b, gate, su, vn, gla_norm_g.reshape(L, 1, GLA_V), ws_b, bs_full, wo_b,
                      norm_mlp_g.reshape(L, 1, D), w1_b, w2_b,
                      fg=final_norm_g.reshape(1, D) if l == L - 1 else None)
    return xf.reshape(B, S, D)
```
